```python
import math
import jax, jax.numpy as jnp
from jax import lax
import numpy as np

D_MODEL = 2048
BATCH = 4
SEQ = 8192
DEPTH = 1
DEC_BATCH = 8
DEC_SEQ = 4096
PAST_LEN = 128

N_META = 16
N_HEADS = 8
HEAD_DIM = 64
V_DIM = 2 * HEAD_DIM
D_ATT = N_HEADS * V_DIM
D_QK = N_HEADS * 2 * HEAD_DIM
D_CONV = D_MODEL - D_ATT
D_IN = 2 * D_QK + D_ATT + 2 * D_CONV
ROT_DIM = HEAD_DIM // 4
ROPE_THETA = 500000.0
CONV_WIDTH = 31
D_FF = 5632
FFN_CONV_WIDTH = 3
Q_BLOCK = 128
LN_EPS = 1e-5
DEEPNORM_ALPHA = (2.0 * DEPTH) ** 0.25
DEEPNORM_BETA = (8.0 * DEPTH) ** -0.25

kernel_name = "hybrid_diffattn_conformer_encoder"


def _lambda_init(layer_idx):
    return 0.8 - 0.6 * math.exp(-0.3 * layer_idx)


def _layer_norm(x, g, b):
    xf = x.astype(jnp.float32)
    mu = jnp.mean(xf, axis=-1, keepdims=True)
    var = jnp.mean(jnp.square(xf - mu), axis=-1, keepdims=True)
    y = (xf - mu) * lax.rsqrt(var + LN_EPS) * g.astype(jnp.float32) + b.astype(jnp.float32)
    return y.astype(x.dtype)


def _rms_norm(x, g):
    xf = x.astype(jnp.float32)
    y = xf * lax.rsqrt(jnp.mean(jnp.square(xf), axis=-1, keepdims=True) + LN_EPS) * g.astype(jnp.float32)
    return y.astype(x.dtype)


def _partial_rope(x, pos):
    inv_freq = ROPE_THETA ** (-jnp.arange(0, ROT_DIM, 2, dtype=jnp.float32) / ROT_DIM)
    ang = pos.astype(jnp.float32)[:, None] * inv_freq[None, :]
    ang = jnp.concatenate([ang, ang], axis=-1)[None, :, None, None, :]
    cos, sin = jnp.cos(ang), jnp.sin(ang)
    xr = x[..., :ROT_DIM].astype(jnp.float32)
    x1, x2 = xr[..., : ROT_DIM // 2], xr[..., ROT_DIM // 2:]
    rot = xr * cos + jnp.concatenate([-x2, x1], axis=-1) * sin
    return jnp.concatenate([rot.astype(x.dtype), x[..., ROT_DIM:]], axis=-1)


def _depthwise_conv(x, w, b):
    k = w.shape[0]
    y = lax.conv_general_dilated(
        x, w[:, None, :].astype(x.dtype), window_strides=(1,),
        padding=[(k // 2, k // 2)], dimension_numbers=("NWC", "WIO", "NWC"),
        feature_group_count=x.shape[-1])
    return y + b


def _diff_attention(q, k, v, lam):
    bsz, seq_len = q.shape[0], q.shape[1]
    n_blocks = -(-seq_len // Q_BLOCK)
    pad = n_blocks * Q_BLOCK - seq_len
    qp = jnp.pad(q, ((0, 0), (0, pad), (0, 0), (0, 0), (0, 0)))
    qb = qp.reshape(bsz, n_blocks, Q_BLOCK, N_HEADS, 2, HEAD_DIM).transpose(1, 0, 2, 3, 4, 5)
    scale = HEAD_DIM ** -0.5

    def block(q_blk):
        s = jnp.einsum("bqhcd,bkhcd->bhcqk", q_blk, k).astype(jnp.float32) * scale
        p = jax.nn.softmax(s, axis=-1)
        w = p[:, :, 0] - lam * p[:, :, 1]
        return jnp.einsum("bhqk,bkhe->bqhe", w.astype(v.dtype), v)

    out = lax.map(block, qb)
    out = out.transpose(1, 0, 2, 3, 4).reshape(bsz, n_blocks * Q_BLOCK, N_HEADS, V_DIM)
    return out[:, :seq_len]


def _trunk(x, meta_tokens, ln_emb_g, ln_emb_b, w_in, b_in, lambda_q1, lambda_k1, lambda_q2, lambda_k2,
           subln_g, conv_w, conv_b, conv_ln_g, conv_ln_b, w_out, b_out, ln1_g, ln1_b,
           w_up, b_up, ffn_conv_w, ffn_conv_b, w_down, b_down, ln2_g, ln2_b):
    bsz = x.shape[0]
    meta = jnp.broadcast_to(meta_tokens[None].astype(x.dtype), (bsz, N_META, D_MODEL))
    h = jnp.concatenate([meta, x], axis=1)
    h = _layer_norm(h, ln_emb_g, ln_emb_b)
    seq_len = h.shape[1]
    pos = jnp.arange(seq_len, dtype=jnp.int32)
    for l in range(DEPTH):
        lam_init = _lambda_init(l)
        proj = h @ w_in[l] + b_in[l]
        q = proj[..., :D_QK].reshape(bsz, seq_len, N_HEADS, 2, HEAD_DIM)
        k = proj[..., D_QK:2 * D_QK].reshape(bsz, seq_len, N_HEADS, 2, HEAD_DIM)
        v = proj[..., 2 * D_QK:2 * D_QK + D_ATT].reshape(bsz, seq_len, N_HEADS, V_DIM)
        u = proj[..., 2 * D_QK + D_ATT:]
        q = _partial_rope(q, pos)
        k = _partial_rope(k, pos)
        lam = (jnp.exp(jnp.sum(lambda_q1[l].astype(jnp.float32) * lambda_k1[l].astype(jnp.float32)))
               - jnp.exp(jnp.sum(lambda_q2[l].astype(jnp.float32) * lambda_k2[l].astype(jnp.float32)))
               + lam_init)
        att = _diff_attention(q, k, v, lam)
        att = (_rms_norm(att, subln_g[l]) * (1.0 - lam_init)).reshape(bsz, seq_len, D_ATT)
        c = u[..., :D_CONV] * jax.nn.sigmoid(u[..., D_CONV:])
        c = _depthwise_conv(c, conv_w[l], conv_b[l])
        c = jax.nn.silu(_layer_norm(c, conv_ln_g[l], conv_ln_b[l]))
        mix = jnp.concatenate([att, c], axis=-1) @ w_out[l] + b_out[l]
        h = _layer_norm(DEEPNORM_ALPHA * h + mix, ln1_g[l], ln1_b[l])
        up = _depthwise_conv(h @ w_up[l] + b_up[l], ffn_conv_w[l], ffn_conv_b[l])
        f = jax.nn.silu(up[..., :D_FF]) * up[..., D_FF:]
        ffn = f @ w_down[l] + b_down[l]
        h = _layer_norm(DEEPNORM_ALPHA * h + ffn, ln2_g[l], ln2_b[l])
    return h[:, N_META:]


def setup_inputs(seed: int = 0) -> dict:
    key = jax.random.key(seed)
    ks = jax.random.split(key, 32)
    f32 = jnp.float32

    def nrm(k, shape, scale):
        return jax.random.normal(k, shape, f32) * scale

    def gain(k, shape):
        return 1.0 + 0.01 * jax.random.normal(k, shape, f32)

    return {
        "x_prompt": nrm(ks[0], (BATCH, SEQ, D_MODEL), 1.0),
        "x_sample": nrm(ks[1], (DEC_BATCH, DEC_SEQ, D_MODEL), 1.0),
        "meta_tokens": nrm(ks[2], (N_META, D_MODEL), 1.0),
        "ln_emb_g": gain(ks[3], (D_MODEL,)),
        "ln_emb_b": nrm(ks[4], (D_MODEL,), 0.01),
        "w_in": nrm(ks[5], (DEPTH, D_MODEL, D_IN), D_MODEL ** -0.5),
        "b_in": nrm(ks[6], (DEPTH, D_IN), 0.01),
        "lambda_q1": nrm(ks[7], (DEPTH, HEAD_DIM), 0.1),
        "lambda_k1": nrm(ks[8], (DEPTH, HEAD_DIM), 0.1),
        "lambda_q2": nrm(ks[9], (DEPTH, HEAD_DIM), 0.1),
        "lambda_k2": nrm(ks[10], (DEPTH, HEAD_DIM), 0.1),
        "subln_g": gain(ks[11], (DEPTH, V_DIM)),
        "conv_w": nrm(ks[12], (DEPTH, CONV_WIDTH, D_CONV), CONV_WIDTH ** -0.5),
        "conv_b": nrm(ks[13], (DEPTH, D_CONV), 0.01),
        "conv_ln_g": gain(ks[14], (DEPTH, D_CONV)),
        "conv_ln_b": nrm(ks[15], (DEPTH, D_CONV), 0.01),
        "w_out": nrm(ks[16], (DEPTH, D_MODEL, D_MODEL), DEEPNORM_BETA * D_MODEL ** -0.5),
        "b_out": nrm(ks[17], (DEPTH, D_MODEL), 0.01),
        "ln1_g": gain(ks[18], (DEPTH, D_MODEL)),
        "ln1_b": nrm(ks[19], (DEPTH, D_MODEL), 0.01),
        "w_up": nrm(ks[20], (DEPTH, D_MODEL, 2 * D_FF), D_MODEL ** -0.5),
        "b_up": nrm(ks[21], (DEPTH, 2 * D_FF), 0.01),
        "ffn_conv_w": nrm(ks[22], (DEPTH, FFN_CONV_WIDTH, 2 * D_FF), FFN_CONV_WIDTH ** -0.5),
        "ffn_conv_b": nrm(ks[23], (DEPTH, 2 * D_FF), 0.01),
        "w_down": nrm(ks[24], (DEPTH, D_FF, D_MODEL), DEEPNORM_BETA * D_FF ** -0.5),
        "b_down": nrm(ks[25], (DEPTH, D_MODEL), 0.01),
        "ln2_g": gain(ks[26], (DEPTH, D_MODEL)),
        "ln2_b": nrm(ks[27], (DEPTH, D_MODEL), 0.01),
    }


def reference(x_prompt, x_sample, meta_tokens, ln_emb_g, ln_emb_b, w_in, b_in, lambda_q1, lambda_k1,
              lambda_q2, lambda_k2, subln_g, conv_w, conv_b, conv_ln_g, conv_ln_b, w_out, b_out,
              ln1_g, ln1_b, w_up, b_up, ffn_conv_w, ffn_conv_b, w_down, b_down, ln2_g, ln2_b):
    params = (meta_tokens, ln_emb_g, ln_emb_b, w_in, b_in, lambda_q1, lambda_k1, lambda_q2, lambda_k2,
              subln_g, conv_w, conv_b, conv_ln_g, conv_ln_b, w_out, b_out, ln1_g, ln1_b,
              w_up, b_up, ffn_conv_w, ffn_conv_b, w_down, b_down, ln2_g, ln2_b)
    y_prompt = _trunk(x_prompt, *params)
    y_sample = _trunk(x_sample, *params)
    return (y_prompt, y_sample)
```

```python
import functools
import math

import jax
import jax.numpy as jnp
from jax import lax
from jax.experimental import pallas as pl
from jax.experimental.pallas import tpu as pltpu

F32 = jnp.float32
BF16 = jnp.bfloat16

N_META = 16
N_HEADS = 8
HEAD_DIM = 64
V_DIM = 2 * HEAD_DIM
D_ATT = N_HEADS * V_DIM
D_QK = N_HEADS * 2 * HEAD_DIM
ROT_DIM = HEAD_DIM // 4
ROPE_THETA = 500000.0
CONV_WIDTH = 31
CONV_HALF = CONV_WIDTH // 2
FFN_CONV_WIDTH = 3
LN_EPS = 1e-5
DEPTH = 1
DEEPNORM_ALPHA = (2.0 * DEPTH) ** 0.25
LAMBDA_INIT = 0.8 - 0.6 * math.exp(-0.3 * 0)

LANES = 128
SUBLANES_F32 = 8
SUBLANES_BF16 = 16
META_ROWS = LANES
HALO = SUBLANES_BF16
NEG_BIG = -1e30
VMEM_LIMIT = 56 * 1024 * 1024


def _ln(x, g, b):
    mu = jnp.mean(x, axis=-1, keepdims=True)
    xc = x - mu
    var = jnp.mean(xc * xc, axis=-1, keepdims=True)
    return xc * lax.rsqrt(var + LN_EPS) * g + b


def _sigmoid(x):
    return 1.0 / (1.0 + jnp.exp(-x))


def _inproj_kernel(x_ref, ge_ref, be_ref, w_ref, b_ref, cos_ref, sa_ref, sb_ref,
                   qT_ref, k_ref, vT_ref, c_ref, h0_sc, u1_sc):
    j = pl.program_id(2)

    @pl.when(j == 0)
    def _():
        h0_sc[...] = _ln(x_ref[0], ge_ref[...], be_ref[...]).astype(BF16)

    p = jnp.dot(h0_sc[...], w_ref[...], preferred_element_type=F32) + b_ref[...]

    def rope(y):
        cos, sa, sb = cos_ref[...], sa_ref[...], sb_ref[...]
        outs = []
        for blk in range(y.shape[1] // LANES):
            yb = y[:, blk * LANES:(blk + 1) * LANES]
            up = pltpu.roll(yb, LANES - ROT_DIM // 2, 1)
            dn = pltpu.roll(yb, ROT_DIM // 2, 1)
            outs.append(yb * cos + up * sa + dn * sb)
        return jnp.concatenate(outs, axis=1)

    @pl.when(j == 0)
    def _():
        q = rope(p) * (HEAD_DIM ** -0.5)
        qT_ref[0] = q.T.astype(BF16)

    @pl.when(j == 1)
    def _():
        k_ref[0] = rope(p).astype(BF16)

    @pl.when(j == 2)
    def _():
        vT_ref[0] = p.T.astype(BF16)

    @pl.when(j == 3)
    def _():
        u1_sc[...] = p

    @pl.when(j == 4)
    def _():
        c_ref[0] = (u1_sc[...] * _sigmoid(p)).astype(BF16)


def _inproj(x, pos0, ge, be, w_in, b_in, *, tm):
    bsz, t, d = x.shape
    n_chunk = w_in.shape[1] // D_QK
    pos = (pos0 + jnp.arange(t, dtype=jnp.int32)).astype(F32)
    inv_freq = ROPE_THETA ** (-jnp.arange(0, ROT_DIM, 2, dtype=F32) / ROT_DIM)
    ang = pos[:, None] * inv_freq[None, :]
    cos, sin = jnp.cos(ang), jnp.sin(ang)
    half = ROT_DIM // 2
    one = jnp.ones((t, HEAD_DIM - ROT_DIM), F32)
    zero_r = jnp.zeros((t, HEAD_DIM - ROT_DIM), F32)
    zero_h = jnp.zeros((t, half), F32)
    cos64 = jnp.concatenate([cos, cos, one], axis=1)
    sa64 = jnp.concatenate([-sin, zero_h, zero_r], axis=1)
    sb64 = jnp.concatenate([zero_h, sin, zero_r], axis=1)
    rep = LANES // HEAD_DIM
    cos_t, sa_t, sb_t = (jnp.tile(a, (1, rep)) for a in (cos64, sa64, sb64))

    row_spec = pl.BlockSpec((1, tm, D_QK), lambda b, i, j: (b, i, 0))
    col_spec = pl.BlockSpec((1, D_QK, tm), lambda b, i, j: (b, 0, i))
    tab_spec = pl.BlockSpec((tm, LANES), lambda b, i, j: (i, 0))
    vec_d = pl.BlockSpec((1, d), lambda b, i, j: (0, 0))
    return pl.pallas_call(
        _inproj_kernel,
        grid=(bsz, t // tm, n_chunk),
        in_specs=[
            pl.BlockSpec((1, tm, d), lambda b, i, j: (b, i, 0)),
            vec_d, vec_d,
            pl.BlockSpec((d, D_QK), lambda b, i, j: (0, j)),
            pl.BlockSpec((1, D_QK), lambda b, i, j: (0, j)),
            tab_spec, tab_spec, tab_spec,
        ],
        out_specs=[col_spec, row_spec, col_spec, row_spec],
        out_shape=[
            jax.ShapeDtypeStruct((bsz, D_QK, t), BF16),
            jax.ShapeDtypeStruct((bsz, t, D_QK), BF16),
            jax.ShapeDtypeStruct((bsz, D_ATT, t), BF16),
            jax.ShapeDtypeStruct((bsz, t, D_QK), BF16),
        ],
        scratch_shapes=[pltpu.VMEM((tm, d), BF16), pltpu.VMEM((tm, D_QK), F32)],
        compiler_params=pltpu.CompilerParams(
            dimension_semantics=("parallel", "parallel", "arbitrary"),
            vmem_limit_bytes=VMEM_LIMIT),
        name="inproj",
    )(x, ge, be, w_in, b_in, cos_t, sa_t, sb_t)


def _attn_kernel(qT_ref, k_ref, vT_ref, km_ref, vTm_ref, lq1_ref, lk1_ref, lq2_ref, lk2_ref,
                 g_ref, o_ref, qm_sc, m_sc, l_sc, acc_sc, *, tq, nkv):
    j = pl.program_id(3)

    @pl.when(j == 0)
    def _():
        qT = qT_ref[0].astype(F32)
        feat = lax.broadcasted_iota(jnp.int32, qT.shape, 0)
        qm_sc[:, :tq] = jnp.where(feat < HEAD_DIM, qT, 0.0).astype(BF16)
        qm_sc[:, tq:] = jnp.where(feat >= HEAD_DIM, qT, 0.0).astype(BF16)
        s = jnp.dot(km_ref[0], qm_sc[...], preferred_element_type=F32)
        key = lax.broadcasted_iota(jnp.int32, s.shape, 0)
        s = jnp.where(key < N_META, s, NEG_BIG)
        m = jnp.max(s, axis=0, keepdims=True)
        p = jnp.exp(s - m)
        m_sc[...] = m
        l_sc[...] = jnp.sum(p, axis=0, keepdims=True)
        acc_sc[...] = jnp.dot(vTm_ref[0], p.astype(BF16), preferred_element_type=F32)

    s = jnp.dot(k_ref[0], qm_sc[...], preferred_element_type=F32)
    m_old = m_sc[...]
    m_new = jnp.maximum(m_old, jnp.max(s, axis=0, keepdims=True))
    alpha = jnp.exp(m_old - m_new)
    p = jnp.exp(s - m_new)
    l_sc[...] = alpha * l_sc[...] + jnp.sum(p, axis=0, keepdims=True)
    acc_sc[...] = alpha * acc_sc[...] + jnp.dot(vT_ref[0], p.astype(BF16),
                                                preferred_element_type=F32)
    m_sc[...] = m_new

    @pl.when(j == nkv - 1)
    def _():
        s1 = jnp.sum(lq1_ref[...] * lk1_ref[...], axis=1, keepdims=True)
        s2 = jnp.sum(lq2_ref[...] * lk2_ref[...], axis=1, keepdims=True)
        lam = jnp.exp(s1) - jnp.exp(s2) + LAMBDA_INIT
        o = acc_sc[...] * (1.0 / l_sc[...])
        o = o[:, :tq] - lam * o[:, tq:]
        ms = jnp.mean(o * o, axis=0, keepdims=True)
        on = o * lax.rsqrt(ms + LN_EPS) * g_ref[...] * (1.0 - LAMBDA_INIT)
        o_ref[0] = on.T.astype(BF16)


def _attention(qT, k, vT, k_meta, vT_meta, lam_vecs, g_col, *, bsz, tq, tk):
    bq, _, t_q = qT.shape
    t = k.shape[1]
    nkv = t // tk
    qb = (lambda b: b) if bq == bsz else (lambda b: 0)
    vec = pl.BlockSpec((1, HEAD_DIM), lambda b, h, i, j: (0, 0))
    return pl.pallas_call(
        functools.partial(_attn_kernel, tq=tq, nkv=nkv),
        grid=(bsz, N_HEADS, t_q // tq, nkv),
        in_specs=[
            pl.BlockSpec((1, V_DIM, tq), lambda b, h, i, j: (qb(b), h, i)),
            pl.BlockSpec((1, tk, V_DIM), lambda b, h, i, j: (b, j, h)),
            pl.BlockSpec((1, V_DIM, tk), lambda b, h, i, j: (b, h, j)),
            pl.BlockSpec((1, META_ROWS, V_DIM), lambda b, h, i, j: (0, 0, h)),
            pl.BlockSpec((1, V_DIM, META_ROWS), lambda b, h, i, j: (0, h, 0)),
            vec, vec, vec, vec,
            pl.BlockSpec((V_DIM, 1), lambda b, h, i, j: (0, 0)),
        ],
        out_specs=pl.BlockSpec((1, tq, V_DIM), lambda b, h, i, j: (b, i, h)),
        out_shape=jax.ShapeDtypeStruct((bsz, t_q, D_ATT), BF16),
        scratch_shapes=[
            pltpu.VMEM((V_DIM, 2 * tq), BF16),
            pltpu.VMEM((1, 2 * tq), F32),
            pltpu.VMEM((1, 2 * tq), F32),
            pltpu.VMEM((V_DIM, 2 * tq), F32),
        ],
        compiler_params=pltpu.CompilerParams(
            dimension_semantics=("parallel", "parallel", "parallel", "arbitrary"),
            vmem_limit_bytes=VMEM_LIMIT),
        name="attention",
    )(qT, k, vT, k_meta, vT_meta, *lam_vecs, g_col)


def _mix_kernel(x_ref, ge_ref, be_ref, att_ref, cmain_ref, cprev_ref, cnext_ref, cfirst_ref,
                cw_ref, cb_ref, cg_ref, cbeta_ref, wo_ref, bo_ref, g1_ref, b1_ref,
                h1_ref, win_sc, conv_sc, *, tm, n_tiles, zero_right_at_end, row_chunk):
    i = pl.program_id(1)
    left = jnp.where(i == 0, cfirst_ref[0].astype(F32), cprev_ref[0].astype(F32))
    right = cnext_ref[0].astype(F32)
    if zero_right_at_end:
        right = jnp.where(i == n_tiles - 1, 0.0, right)
    base = HALO - CONV_HALF
    n_lane = conv_sc.shape[1] // LANES
    for lc in range(n_lane):
        lanes = slice(lc * LANES, (lc + 1) * LANES)
        win_sc[lc, 0:HALO, :] = left[:, lanes]
        win_sc[lc, HALO:HALO + tm, :] = cmain_ref[0, :, lanes].astype(F32)
        win_sc[lc, HALO + tm:, :] = right[:, lanes]

    for lc in range(n_lane):
        lanes = slice(lc * LANES, (lc + 1) * LANES)

        def body(rc, carry, lanes=lanes, lc=lc):
            r0 = pl.multiple_of(rc * row_chunk, row_chunk)
            view = win_sc.at[lc, pl.ds(r0, row_chunk + 2 * HALO), :]
            acc = jnp.zeros((row_chunk, LANES), F32) + cb_ref[:, lanes]
            for t in range(CONV_WIDTH):
                acc = acc + cw_ref[t:t + 1, lanes] * view[base + t:base + t + row_chunk, :]
            conv_sc[pl.ds(r0, row_chunk), lanes] = acc
            return carry

        lax.fori_loop(0, tm // row_chunk, body, 0)

    y = _ln(conv_sc[...], cg_ref[...], cbeta_ref[...])
    cc = (y * _sigmoid(y)).astype(BF16)
    mix = (jnp.dot(att_ref[0], wo_ref[0:D_ATT, :], preferred_element_type=F32)
           + jnp.dot(cc, wo_ref[D_ATT:, :], preferred_element_type=F32) + bo_ref[...])
    h0 = _ln(x_ref[0], ge_ref[...], be_ref[...])
    h1_ref[0] = _ln(DEEPNORM_ALPHA * h0 + mix, g1_ref[...], b1_ref[...])


def _mix(x, att, c_main, c_halo, c_first, ge, be, conv_w, conv_b, conv_g, conv_beta,
         w_out, b_out, g1, b1, *, bsz, t, tm, xb, ab, cmb, zero_right_at_end):
    d = x.shape[-1]
    n_tiles = t // tm
    hb = tm // HALO
    n_hblk = c_halo.shape[1] // HALO
    row_chunk = min(tm, 64)
    vec_c = pl.BlockSpec((1, D_ATT), lambda b, i: (0, 0))
    vec_d = pl.BlockSpec((1, d), lambda b, i: (0, 0))
    return pl.pallas_call(
        functools.partial(_mix_kernel, tm=tm, n_tiles=n_tiles,
                          zero_right_at_end=zero_right_at_end, row_chunk=row_chunk),
        grid=(bsz, n_tiles),
        in_specs=[
            pl.BlockSpec((1, tm, d), lambda b, i: (xb(b), i, 0)),
            vec_d, vec_d,
            pl.BlockSpec((1, tm, D_ATT), lambda b, i: (ab(b), i, 0)),
            pl.BlockSpec((1, tm, D_ATT), lambda b, i: (cmb(b), i, 0)),
            pl.BlockSpec((1, HALO, D_ATT), lambda b, i: (b, jnp.maximum(i * hb - 1, 0), 0)),
            pl.BlockSpec((1, HALO, D_ATT),
                         lambda b, i: (b, jnp.minimum((i + 1) * hb, n_hblk - 1), 0))
            if zero_right_at_end else
            pl.BlockSpec((1, HALO, D_ATT), lambda b, i: (b, 0, 0)),
            pl.BlockSpec((1, HALO, D_ATT), lambda b, i: (0, 0, 0)),
            pl.BlockSpec((CONV_WIDTH, D_ATT), lambda b, i: (0, 0)),
            vec_c, vec_c, vec_c,
            pl.BlockSpec((d, d), lambda b, i: (0, 0)),
            vec_d, vec_d, vec_d,
        ],
        out_specs=pl.BlockSpec((1, tm, d), lambda b, i: (b, i, 0)),
        out_shape=jax.ShapeDtypeStruct((bsz, t, d), F32),
        scratch_shapes=[pltpu.VMEM((D_ATT // LANES, tm + 2 * HALO, LANES), F32),
                        pltpu.VMEM((tm, D_ATT), F32)],
        compiler_params=pltpu.CompilerParams(
            dimension_semantics=("parallel", "parallel"),
            vmem_limit_bytes=VMEM_LIMIT),
        name="mix",
    )(x, ge, be, att, c_main, c_halo, c_halo, c_first, conv_w, conv_b, conv_g, conv_beta,
      w_out, b_out, g1, b1)


def _ffn_kernel(h_ref, hprev_ref, hnext_ref, hmeta_ref, wg_ref, wv_ref, bg_ref, bv_ref,
                cwg_ref, cwv_ref, cbg_ref, cbv_ref, wd_ref, bd_ref, g2_ref, b2_ref,
                y_ref, hb_sc, acc_sc, *, tm, n_tiles, nj):
    i = pl.program_id(1)
    j = pl.program_id(2)
    last_tile = i == n_tiles - 1

    @pl.when(j == 0)
    def _():
        hb_sc[0:tm, :] = h_ref[0].astype(BF16)
        lastrow = SUBLANES_F32 - 1
        prev_row = jnp.where(i == 0, hmeta_ref[0, lastrow:lastrow + 1, :],
                             hprev_ref[0, lastrow:lastrow + 1, :])
        next_row = hnext_ref[0, 0:1, :]
        r = lax.broadcasted_iota(jnp.int32, (SUBLANES_BF16, h_ref.shape[-1]), 0)
        halo = jnp.where(r == 0, prev_row, jnp.where(r == 1, next_row, 0.0))
        hb_sc[tm:, :] = halo.astype(BF16)
        acc_sc[...] = jnp.zeros_like(acc_sc)

    hb = hb_sc[...]
    row = lax.broadcasted_iota(jnp.int32, (tm, wg_ref.shape[1]), 0)
    next_keep = jnp.where(last_tile, 0.0, 1.0)

    def conv3(w_ref, b_ref, cw_ref, cb_ref):
        u = jnp.dot(hb, w_ref[...], preferred_element_type=F32) + b_ref[...]
        main = u[0:tm]
        prow = u[tm:tm + 1]
        nrow = u[tm + 1:tm + 2] * next_keep
        below = jnp.where(row == 0, prow, pltpu.roll(main, 1, 0))
        above = jnp.where(row == tm - 1, nrow, pltpu.roll(main, tm - 1, 0))
        return cw_ref[0:1, :] * below + cw_ref[1:2, :] * main + cw_ref[2:3, :] * above + cb_ref[...]

    gate = conv3(wg_ref, bg_ref, cwg_ref, cbg_ref)
    val = conv3(wv_ref, bv_ref, cwv_ref, cbv_ref)
    f = (gate * _sigmoid(gate) * val).astype(BF16)
    acc_sc[...] += jnp.dot(f, wd_ref[...], preferred_element_type=F32)

    @pl.when(j == nj - 1)
    def _():
        y_ref[0] = _ln(DEEPNORM_ALPHA * h_ref[0] + acc_sc[...] + bd_ref[...],
                       g2_ref[...], b2_ref[...])


def _ffn(h1, h1_meta, w_up, b_up, cw, cb, w_down, b_down, g2, b2, *, tm, tn):
    bsz, t, d = h1.shape
    d_ff = w_down.shape[0]
    nj = d_ff // tn
    n_tiles = t // tm
    hb = tm // SUBLANES_F32
    n_hblk = t // SUBLANES_F32
    vec_d = pl.BlockSpec((1, d), lambda b, i, j: (0, 0))
    gate_cols = lambda b, i, j: (0, j)
    val_cols = lambda b, i, j: (0, nj + j)
    return pl.pallas_call(
        functools.partial(_ffn_kernel, tm=tm, n_tiles=n_tiles, nj=nj),
        grid=(bsz, n_tiles, nj),
        in_specs=[
            pl.BlockSpec((1, tm, d), lambda b, i, j: (b, i, 0)),
            pl.BlockSpec((1, SUBLANES_F32, d), lambda b, i, j: (b, jnp.maximum(i * hb - 1, 0), 0)),
            pl.BlockSpec((1, SUBLANES_F32, d),
                         lambda b, i, j: (b, jnp.minimum((i + 1) * hb, n_hblk - 1), 0)),
            pl.BlockSpec((1, SUBLANES_F32, d), lambda b, i, j: (b, N_META // SUBLANES_F32 - 1, 0)),
            pl.BlockSpec((d, tn), gate_cols), pl.BlockSpec((d, tn), val_cols),
            pl.BlockSpec((1, tn), gate_cols), pl.BlockSpec((1, tn), val_cols),
            pl.BlockSpec((FFN_CONV_WIDTH, tn), gate_cols), pl.BlockSpec((FFN_CONV_WIDTH, tn), val_cols),
            pl.BlockSpec((1, tn), gate_cols), pl.BlockSpec((1, tn), val_cols),
            pl.BlockSpec((tn, d), lambda b, i, j: (j, 0)),
            vec_d, vec_d, vec_d,
        ],
        out_specs=pl.BlockSpec((1, tm, d), lambda b, i, j: (b, i, 0)),
        out_shape=jax.ShapeDtypeStruct((bsz, t, d), F32),
        scratch_shapes=[pltpu.VMEM((tm + SUBLANES_BF16, d), BF16), pltpu.VMEM((tm, d), F32)],
        compiler_params=pltpu.CompilerParams(
            dimension_semantics=("parallel", "parallel", "arbitrary"),
            vmem_limit_bytes=VMEM_LIMIT),
        name="ffn",
    )(h1, h1, h1, h1_meta, w_up, w_up, b_up, b_up, cw, cw, cb, cb, w_down, b_down, g2, b2)


TILE_M = 512
TILE_Q = 512
TILE_K = 1024
TILE_FF = 512


def _trunk(x, meta_frame, p):
    bsz, t, d = x.shape
    tm, tq, tk = min(t, TILE_M), min(t, TILE_Q), min(t, TILE_K)
    assert t % tm == 0 and t % tq == 0 and t % tk == 0, (t, tm, tq, tk)
    qT_m, k_m, vT_m, c_m = meta_frame
    ident = lambda b: b
    zero = lambda b: 0

    qT, k, vT, c = _inproj(x, N_META, p["ge"], p["be"], p["w_in"], p["b_in"], tm=tm)
    att = _attention(qT, k, vT, k_m, vT_m, p["lam"], p["subln_g"], bsz=bsz, tq=tq, tk=tk)
    att_m = _attention(qT_m, k, vT, k_m, vT_m, p["lam"], p["subln_g"],
                       bsz=bsz, tq=META_ROWS, tk=tk)
    mix_w = (p["ge"], p["be"], p["conv_w"], p["conv_b"], p["conv_g"], p["conv_beta"],
             p["w_out"], p["b_out"], p["g1"], p["b1"])
    h1 = _mix(x, att, c, c, c_m, *mix_w, bsz=bsz, t=t, tm=tm,
              xb=ident, ab=ident, cmb=ident, zero_right_at_end=True)
    h1_m = _mix(p["meta"], att_m, c_m, c, jnp.zeros((1, HALO, D_ATT), BF16), *mix_w,
                bsz=bsz, t=N_META, tm=N_META, xb=zero, ab=ident, cmb=zero,
                zero_right_at_end=False)
    return _ffn(h1, h1_m, p["w_up"], p["b_up"], p["ffn_cw"], p["ffn_cb"], p["w_down"],
                p["b_down"], p["g2"], p["b2"], tm=tm, tn=TILE_FF)


def kernel(x_prompt, x_sample, meta_tokens, ln_emb_g, ln_emb_b, w_in, b_in, lambda_q1, lambda_k1,
           lambda_q2, lambda_k2, subln_g, conv_w, conv_b, conv_ln_g, conv_ln_b, w_out, b_out,
           ln1_g, ln1_b, w_up, b_up, ffn_conv_w, ffn_conv_b, w_down, b_down, ln2_g, ln2_b):
    l = 0
    row = lambda v: v.reshape(1, -1).astype(F32)
    p = dict(
        meta=meta_tokens[None].astype(F32),
        ge=row(ln_emb_g), be=row(ln_emb_b),
        w_in=w_in[l].astype(BF16), b_in=row(b_in[l]),
        lam=(row(lambda_q1[l]), row(lambda_k1[l]), row(lambda_q2[l]), row(lambda_k2[l])),
        subln_g=subln_g[l].reshape(-1, 1).astype(F32),
        conv_w=conv_w[l].astype(F32), conv_b=row(conv_b[l]),
        conv_g=row(conv_ln_g[l]), conv_beta=row(conv_ln_b[l]),
        w_out=w_out[l].astype(BF16), b_out=row(b_out[l]), g1=row(ln1_g[l]), b1=row(ln1_b[l]),
        w_up=w_up[l].astype(BF16), b_up=row(b_up[l]),
        ffn_cw=ffn_conv_w[l].astype(F32), ffn_cb=row(ffn_conv_b[l]),
        w_down=w_down[l].astype(BF16), b_down=row(b_down[l]), g2=row(ln2_g[l]), b2=row(ln2_b[l]),
    )
    d = meta_tokens.shape[-1]
    meta_pad = jnp.zeros((1, META_ROWS, d), F32).at[0, :N_META].set(meta_tokens.astype(F32))
    meta_frame = _inproj(meta_pad, 0, p["ge"], p["be"], p["w_in"], p["b_in"], tm=META_ROWS)
    return (_trunk(x_prompt, meta_frame, p), _trunk(x_sample, meta_frame, p))
```

```python
import functools
import math

import jax
import jax.numpy as jnp
from jax import lax
from jax.experimental import pallas as pl
from jax.experimental.pallas import tpu as pltpu

F32 = jnp.float32
BF16 = jnp.bfloat16

N_META = 16
N_HEADS = 8
HEAD_DIM = 64
V_DIM = 2 * HEAD_DIM
D_ATT = N_HEADS * V_DIM
D_QK = N_HEADS * 2 * HEAD_DIM
ROT_DIM = HEAD_DIM // 4
ROPE_THETA = 500000.0
CONV_WIDTH = 31
CONV_HALF = CONV_WIDTH // 2
FFN_CONV_WIDTH = 3
LN_EPS = 1e-5
DEPTH = 1
DEEPNORM_ALPHA = (2.0 * DEPTH) ** 0.25
LAMBDA_INIT = 0.8 - 0.6 * math.exp(-0.3 * 0)

LANES = 128
SUBLANES_F32 = 8
SUBLANES_BF16 = 16
META_ROWS = LANES
HALO = SUBLANES_BF16
MXU_WIDTH = 256
COL_CHUNK = MXU_WIDTH
LOG2E = math.log2(math.e)
NEG_BIG = -1e30
VMEM_LIMIT = 56 * 1024 * 1024


def _ln(x, g, b):
    mu = jnp.mean(x, axis=-1, keepdims=True)
    xc = x - mu
    var = jnp.mean(xc * xc, axis=-1, keepdims=True)
    return xc * lax.rsqrt(var + LN_EPS) * g + b


def _sigmoid(x):
    return 1.0 / (1.0 + jnp.exp(-x))


def _inproj_kernel(x_ref, ge_ref, be_ref, w_ref, b_ref, cos_ref, sa_ref, sb_ref,
                   q_ref, k_ref, v_ref, c_ref, h0_sc, pa_sc, pb_sc):
    h0_sc[...] = _ln(x_ref[0], ge_ref[...], be_ref[...]).astype(BF16)

    def project(chunk, dst):
        cols = slice(chunk * D_QK, (chunk + 1) * D_QK)
        dst[...] = (jnp.dot(h0_sc[...], w_ref[:, cols], preferred_element_type=F32)
                    + b_ref[:, cols])

    def rope(y):
        cos, sa, sb = cos_ref[...], sa_ref[...], sb_ref[...]
        outs = []
        for blk in range(y.shape[1] // LANES):
            yb = y[:, blk * LANES:(blk + 1) * LANES]
            up = pltpu.roll(yb, LANES - ROT_DIM // 2, 1)
            dn = pltpu.roll(yb, ROT_DIM // 2, 1)
            outs.append(yb * cos + up * sa + dn * sb)
        return jnp.concatenate(outs, axis=1)

    project(0, pa_sc)
    project(1, pb_sc)
    q_ref[0] = (rope(pa_sc[...]) * (HEAD_DIM ** -0.5 * LOG2E)).astype(BF16)
    project(2, pa_sc)
    k_ref[0] = rope(pb_sc[...]).astype(BF16)
    project(3, pb_sc)
    v_ref[0] = pa_sc[...].astype(BF16)
    project(4, pa_sc)
    c_ref[0] = (pb_sc[...] * _sigmoid(pa_sc[...])).astype(BF16)


def _inproj(x, pos0, ge, be, w_in, b_in, *, tm):
    bsz, t, d = x.shape
    assert w_in.shape[1] == 5 * D_QK
    pos = (pos0 + jnp.arange(t, dtype=jnp.int32)).astype(F32)
    inv_freq = ROPE_THETA ** (-jnp.arange(0, ROT_DIM, 2, dtype=F32) / ROT_DIM)
    ang = pos[:, None] * inv_freq[None, :]
    cos, sin = jnp.cos(ang), jnp.sin(ang)
    half = ROT_DIM // 2
    one = jnp.ones((t, HEAD_DIM - ROT_DIM), F32)
    zero_r = jnp.zeros((t, HEAD_DIM - ROT_DIM), F32)
    zero_h = jnp.zeros((t, half), F32)
    cos64 = jnp.concatenate([cos, cos, one], axis=1)
    sa64 = jnp.concatenate([-sin, zero_h, zero_r], axis=1)
    sb64 = jnp.concatenate([zero_h, sin, zero_r], axis=1)
    rep = LANES // HEAD_DIM
    cos_t, sa_t, sb_t = (jnp.tile(a, (1, rep)) for a in (cos64, sa64, sb64))

    row_spec = pl.BlockSpec((1, tm, D_QK), lambda b, i: (b, i, 0))
    tab_spec = pl.BlockSpec((tm, LANES), lambda b, i: (i, 0))
    vec_d = pl.BlockSpec((1, d), lambda b, i: (0, 0))
    out = jax.ShapeDtypeStruct((bsz, t, D_QK), BF16)
    return pl.pallas_call(
        _inproj_kernel,
        grid=(bsz, t // tm),
        in_specs=[
            pl.BlockSpec((1, tm, d), lambda b, i: (b, i, 0)),
            vec_d, vec_d,
            pl.BlockSpec(w_in.shape, lambda b, i: (0, 0), pipeline_mode=pl.Buffered(1)),
            pl.BlockSpec(b_in.shape, lambda b, i: (0, 0)),
            tab_spec, tab_spec, tab_spec,
        ],
        out_specs=[row_spec, row_spec, row_spec, row_spec],
        out_shape=[out, out, out, out],
        scratch_shapes=[pltpu.VMEM((tm, d), BF16), pltpu.VMEM((tm, D_QK), F32),
                        pltpu.VMEM((tm, D_QK), F32)],
        compiler_params=pltpu.CompilerParams(
            dimension_semantics=("parallel", "parallel"),
            vmem_limit_bytes=VMEM_LIMIT),
        name="inproj",
    )(x, ge, be, w_in, b_in, cos_t, sa_t, sb_t)


def _attn_kernel(q_ref, k_ref, v_ref, km_ref, vm_ref, lq1_ref, lk1_ref, lq2_ref, lk2_ref,
                 g_ref, o_ref, qm_sc, m_sc, l_sc, acc_sc, sa_sc, sb_sc, mba_sc, mbb_sc,
                 *, tq, tk, nkv):
    qT = q_ref[0].astype(F32).T
    feat = lax.broadcasted_iota(jnp.int32, qT.shape, 0)
    qm_sc[:, :tq] = jnp.where(feat < HEAD_DIM, qT, 0.0).astype(BF16)
    qm_sc[:, tq:] = jnp.where(feat >= HEAD_DIM, qT, 0.0).astype(BF16)

    n_col = (2 * tq) // COL_CHUNK
    vTm = vm_ref[0].T
    for c in range(n_col):
        cols = slice(c * COL_CHUNK, (c + 1) * COL_CHUNK)
        s = jnp.dot(km_ref[0], qm_sc[:, cols], preferred_element_type=F32)
        key = lax.broadcasted_iota(jnp.int32, s.shape, 0)
        s = jnp.where(key < N_META, s, NEG_BIG)
        m = jnp.max(s, axis=0, keepdims=True)
        p = jnp.exp2(s - m)
        m_sc[:, cols] = m
        l_sc[:, cols] = jnp.sum(p, axis=0, keepdims=True)
        acc_sc[:, cols] = jnp.dot(vTm, p.astype(BF16), preferred_element_type=F32)

    def scores(j, s_ref, mb_ref):
        k0 = pl.multiple_of(j * tk, tk)
        kb = k_ref[0, pl.ds(k0, tk), :]
        for c in range(n_col):
            cols = slice(c * COL_CHUNK, (c + 1) * COL_CHUNK)
            s = jnp.dot(kb, qm_sc[:, cols], preferred_element_type=F32)
            s_ref[:, cols] = s
            mb_ref[:, cols] = jnp.max(s, axis=0, keepdims=True)

    def absorb(j, s_ref, mb_ref):
        k0 = pl.multiple_of(j * tk, tk)
        vb = v_ref[0, pl.ds(k0, tk), :].T
        for c in range(n_col):
            cols = slice(c * COL_CHUNK, (c + 1) * COL_CHUNK)
            m_old = m_sc[:, cols]
            m_new = jnp.maximum(m_old, mb_ref[:, cols])
            alpha = jnp.exp2(m_old - m_new)
            p = jnp.exp2(s_ref[:, cols] - m_new)
            l_sc[:, cols] = alpha * l_sc[:, cols] + jnp.sum(p, axis=0, keepdims=True)
            acc_sc[:, cols] = alpha * acc_sc[:, cols] + jnp.dot(
                vb, p.astype(BF16), preferred_element_type=F32)
            m_sc[:, cols] = m_new

    scores(0, sa_sc, mba_sc)

    def kv_pair(jj, carry):
        j = 2 * jj
        scores(j + 1, sb_sc, mbb_sc)
        absorb(j, sa_sc, mba_sc)
        scores(j + 2, sa_sc, mba_sc)
        absorb(j + 1, sb_sc, mbb_sc)
        return carry

    lax.fori_loop(0, nkv // 2 - 1, kv_pair, 0)
    scores(nkv - 1, sb_sc, mbb_sc)
    absorb(nkv - 2, sa_sc, mba_sc)
    absorb(nkv - 1, sb_sc, mbb_sc)

    s1 = jnp.sum(lq1_ref[...] * lk1_ref[...], axis=1, keepdims=True)
    s2 = jnp.sum(lq2_ref[...] * lk2_ref[...], axis=1, keepdims=True)
    lam = jnp.exp(s1) - jnp.exp(s2) + LAMBDA_INIT
    o = acc_sc[...] * (1.0 / l_sc[...])
    o = o[:, :tq] - lam * o[:, tq:]
    ms = jnp.mean(o * o, axis=0, keepdims=True)
    on = o * lax.rsqrt(ms + LN_EPS) * g_ref[...] * (1.0 - LAMBDA_INIT)
    o_ref[0] = on.T.astype(BF16)


def _attention(q, k, v, k_meta, v_meta, lam_vecs, g_col, *, bsz, tq, tk):
    bq, t_q, _ = q.shape
    t = k.shape[1]
    nkv = t // tk
    assert nkv >= 2 and nkv % 2 == 0, "the key loop is pipelined over pairs of key blocks"
    qb = (lambda b: b) if bq == bsz else (lambda b: 0)
    vec = pl.BlockSpec((1, HEAD_DIM), lambda b, h, i: (0, 0))
    return pl.pallas_call(
        functools.partial(_attn_kernel, tq=tq, tk=tk, nkv=nkv),
        grid=(bsz, N_HEADS, t_q // tq),
        in_specs=[
            pl.BlockSpec((1, tq, V_DIM), lambda b, h, i: (qb(b), i, h)),
            pl.BlockSpec((1, t, V_DIM), lambda b, h, i: (b, 0, h)),
            pl.BlockSpec((1, t, V_DIM), lambda b, h, i: (b, 0, h)),
            pl.BlockSpec((1, META_ROWS, V_DIM), lambda b, h, i: (0, 0, h)),
            pl.BlockSpec((1, META_ROWS, V_DIM), lambda b, h, i: (0, 0, h)),
            vec, vec, vec, vec,
            pl.BlockSpec((V_DIM, 1), lambda b, h, i: (0, 0)),
        ],
        out_specs=pl.BlockSpec((1, tq, V_DIM), lambda b, h, i: (b, i, h)),
        out_shape=jax.ShapeDtypeStruct((bsz, t_q, D_ATT), BF16),
        scratch_shapes=[
            pltpu.VMEM((V_DIM, 2 * tq), BF16),
            pltpu.VMEM((1, 2 * tq), F32),
            pltpu.VMEM((1, 2 * tq), F32),
            pltpu.VMEM((V_DIM, 2 * tq), F32),
            pltpu.VMEM((tk, 2 * tq), F32),
            pltpu.VMEM((tk, 2 * tq), F32),
            pltpu.VMEM((1, 2 * tq), F32),
            pltpu.VMEM((1, 2 * tq), F32),
        ],
        compiler_params=pltpu.CompilerParams(
            dimension_semantics=("parallel", "parallel", "parallel"),
            vmem_limit_bytes=VMEM_LIMIT),
        name="attention",
    )(q, k, v, k_meta, v_meta, *lam_vecs, g_col)


def _mix_kernel(x_ref, ge_ref, be_ref, att_ref, cmain_ref, cprev_ref, cnext_ref, cfirst_ref,
                cw_ref, cb_ref, cg_ref, cbeta_ref, wo_ref, bo_ref, g1_ref, b1_ref,
                h1_ref, win_sc, conv_sc, *, tm, n_tiles, zero_right_at_end, row_chunk):
    i = pl.program_id(1)
    left = jnp.where(i == 0, cfirst_ref[0].astype(F32), cprev_ref[0].astype(F32))
    right = cnext_ref[0].astype(F32)
    if zero_right_at_end:
        right = jnp.where(i == n_tiles - 1, 0.0, right)
    base = HALO - CONV_HALF
    n_lane = conv_sc.shape[1] // LANES
    for lc in range(n_lane):
        lanes = slice(lc * LANES, (lc + 1) * LANES)
        win_sc[lc, 0:HALO, :] = left[:, lanes]
        win_sc[lc, HALO:HALO + tm, :] = cmain_ref[0, :, lanes].astype(F32)
        win_sc[lc, HALO + tm:, :] = right[:, lanes]

    for lc in range(n_lane):
        lanes = slice(lc * LANES, (lc + 1) * LANES)

        def body(rc, carry, lanes=lanes, lc=lc):
            r0 = pl.multiple_of(rc * row_chunk, row_chunk)
            view = win_sc.at[lc, pl.ds(r0, row_chunk + 2 * HALO), :]
            acc = jnp.zeros((row_chunk, LANES), F32) + cb_ref[:, lanes]
            for t in range(CONV_WIDTH):
                acc = acc + cw_ref[t:t + 1, lanes] * view[base + t:base + t + row_chunk, :]
            conv_sc[pl.ds(r0, row_chunk), lanes] = acc
            return carry

        lax.fori_loop(0, tm // row_chunk, body, 0)

    y = _ln(conv_sc[...], cg_ref[...], cbeta_ref[...])
    cc = (y * _sigmoid(y)).astype(BF16)
    mix = (jnp.dot(att_ref[0], wo_ref[0:D_ATT, :], preferred_element_type=F32)
           + jnp.dot(cc, wo_ref[D_ATT:, :], preferred_element_type=F32) + bo_ref[...])
    h0 = _ln(x_ref[0], ge_ref[...], be_ref[...])
    h1_ref[0] = _ln(DEEPNORM_ALPHA * h0 + mix, g1_ref[...], b1_ref[...])


def _mix(x, att, c_main, c_halo, c_first, ge, be, conv_w, conv_b, conv_g, conv_beta,
         w_out, b_out, g1, b1, *, bsz, t, tm, xb, ab, cmb, zero_right_at_end):
    d = x.shape[-1]
    n_tiles = t // tm
    hb = tm // HALO
    n_hblk = c_halo.shape[1] // HALO
    row_chunk = min(tm, 64)
    vec_c = pl.BlockSpec((1, D_ATT), lambda b, i: (0, 0))
    vec_d = pl.BlockSpec((1, d), lambda b, i: (0, 0))
    return pl.pallas_call(
        functools.partial(_mix_kernel, tm=tm, n_tiles=n_tiles,
                          zero_right_at_end=zero_right_at_end, row_chunk=row_chunk),
        grid=(bsz, n_tiles),
        in_specs=[
            pl.BlockSpec((1, tm, d), lambda b, i: (xb(b), i, 0)),
            vec_d, vec_d,
            pl.BlockSpec((1, tm, D_ATT), lambda b, i: (ab(b), i, 0)),
            pl.BlockSpec((1, tm, D_ATT), lambda b, i: (cmb(b), i, 0)),
            pl.BlockSpec((1, HALO, D_ATT), lambda b, i: (b, jnp.maximum(i * hb - 1, 0), 0)),
            pl.BlockSpec((1, HALO, D_ATT),
                         lambda b, i: (b, jnp.minimum((i + 1) * hb, n_hblk - 1), 0))
            if zero_right_at_end else
            pl.BlockSpec((1, HALO, D_ATT), lambda b, i: (b, 0, 0)),
            pl.BlockSpec((1, HALO, D_ATT), lambda b, i: (0, 0, 0)),
            pl.BlockSpec((CONV_WIDTH, D_ATT), lambda b, i: (0, 0)),
            vec_c, vec_c, vec_c,
            pl.BlockSpec((d, d), lambda b, i: (0, 0)),
            vec_d, vec_d, vec_d,
        ],
        out_specs=pl.BlockSpec((1, tm, d), lambda b, i: (b, i, 0)),
        out_shape=jax.ShapeDtypeStruct((bsz, t, d), F32),
        scratch_shapes=[pltpu.VMEM((D_ATT // LANES, tm + 2 * HALO, LANES), F32),
                        pltpu.VMEM((tm, D_ATT), F32)],
        compiler_params=pltpu.CompilerParams(
            dimension_semantics=("parallel", "parallel"),
            vmem_limit_bytes=VMEM_LIMIT),
        name="mix",
    )(x, ge, be, att, c_main, c_halo, c_halo, c_first, conv_w, conv_b, conv_g, conv_beta,
      w_out, b_out, g1, b1)


def _ffn_kernel(h_ref, hprev_ref, hnext_ref, hmeta_ref, wg_ref, wv_ref, bg_ref, bv_ref,
                cwg_ref, cwv_ref, cbg_ref, cbv_ref, wd_ref, bd_ref, g2_ref, b2_ref,
                y_ref, hb_sc, acc_sc, ug_sc, uv_sc, *, tm, n_tiles, nj):
    i = pl.program_id(1)
    j = pl.program_id(2)
    pad = SUBLANES_BF16
    d = h_ref.shape[-1]

    @pl.when(j == 0)
    def _():
        lastrow = SUBLANES_F32 - 1
        prev_row = jnp.where(i == 0, hmeta_ref[0, lastrow:lastrow + 1, :],
                             hprev_ref[0, lastrow:lastrow + 1, :])
        next_row = hnext_ref[0, 0:1, :]
        r = lax.broadcasted_iota(jnp.int32, (pad, d), 0)
        hb_sc[0:pad, :] = jnp.where(r == pad - 1, prev_row, 0.0).astype(BF16)
        hb_sc[pad:pad + tm, :] = h_ref[0].astype(BF16)
        hb_sc[pad + tm:, :] = jnp.where(r == 0, next_row, 0.0).astype(BF16)
        acc_sc[...] = jnp.zeros_like(acc_sc)

    hb = hb_sc[...]
    ug_sc[...] = jnp.dot(hb, wg_ref[...], preferred_element_type=F32) + bg_ref[...]
    uv_sc[...] = jnp.dot(hb, wv_ref[...], preferred_element_type=F32) + bv_ref[...]
    keep = jnp.where(i == n_tiles - 1, 0.0, 1.0)
    tail = slice(pad + tm, pad + tm + SUBLANES_F32)
    ug_sc[tail, :] = ug_sc[tail, :] * keep
    uv_sc[tail, :] = uv_sc[tail, :] * keep

    def conv3(u_sc, cw_ref, cb_ref, cols):
        return (cw_ref[0:1, cols] * u_sc[pad - 1:pad - 1 + tm, cols]
                + cw_ref[1:2, cols] * u_sc[pad:pad + tm, cols]
                + cw_ref[2:3, cols] * u_sc[pad + 1:pad + 1 + tm, cols] + cb_ref[:, cols])

    tn = wg_ref.shape[1]
    parts = []
    for c in range(tn // MXU_WIDTH):
        cols = slice(c * MXU_WIDTH, (c + 1) * MXU_WIDTH)
        gate = conv3(ug_sc, cwg_ref, cbg_ref, cols)
        val = conv3(uv_sc, cwv_ref, cbv_ref, cols)
        f = (gate * _sigmoid(gate) * val).astype(BF16)
        parts.append(jnp.dot(f, wd_ref[cols, :], preferred_element_type=F32))
    acc_sc[...] += functools.reduce(lambda a, b: a + b, parts)

    @pl.when(j == nj - 1)
    def _():
        y_ref[0] = _ln(DEEPNORM_ALPHA * h_ref[0] + acc_sc[...] + bd_ref[...],
                       g2_ref[...], b2_ref[...])


def _ffn(h1, h1_meta, w_up, b_up, cw, cb, w_down, b_down, g2, b2, *, tm, tn):
    bsz, t, d = h1.shape
    d_ff = w_down.shape[0]
    nj = d_ff // tn
    n_tiles = t // tm
    hb = tm // SUBLANES_F32
    n_hblk = t // SUBLANES_F32
    vec_d = pl.BlockSpec((1, d), lambda b, i, j: (0, 0))
    gate_cols = lambda b, i, j: (0, j)
    val_cols = lambda b, i, j: (0, nj + j)
    return pl.pallas_call(
        functools.partial(_ffn_kernel, tm=tm, n_tiles=n_tiles, nj=nj),
        grid=(bsz, n_tiles, nj),
        in_specs=[
            pl.BlockSpec((1, tm, d), lambda b, i, j: (b, i, 0)),
            pl.BlockSpec((1, SUBLANES_F32, d), lambda b, i, j: (b, jnp.maximum(i * hb - 1, 0), 0)),
            pl.BlockSpec((1, SUBLANES_F32, d),
                         lambda b, i, j: (b, jnp.minimum((i + 1) * hb, n_hblk - 1), 0)),
            pl.BlockSpec((1, SUBLANES_F32, d), lambda b, i, j: (b, N_META // SUBLANES_F32 - 1, 0)),
            pl.BlockSpec((d, tn), gate_cols), pl.BlockSpec((d, tn), val_cols),
            pl.BlockSpec((1, tn), gate_cols), pl.BlockSpec((1, tn), val_cols),
            pl.BlockSpec((FFN_CONV_WIDTH, tn), gate_cols), pl.BlockSpec((FFN_CONV_WIDTH, tn), val_cols),
            pl.BlockSpec((1, tn), gate_cols), pl.BlockSpec((1, tn), val_cols),
            pl.BlockSpec((tn, d), lambda b, i, j: (j, 0)),
            vec_d, vec_d, vec_d,
        ],
        out_specs=pl.BlockSpec((1, tm, d), lambda b, i, j: (b, i, 0)),
        out_shape=jax.ShapeDtypeStruct((bsz, t, d), F32),
        scratch_shapes=[pltpu.VMEM((tm + 2 * SUBLANES_BF16, d), BF16), pltpu.VMEM((tm, d), F32),
                        pltpu.VMEM((tm + 2 * SUBLANES_BF16, tn), F32),
                        pltpu.VMEM((tm + 2 * SUBLANES_BF16, tn), F32)],
        compiler_params=pltpu.CompilerParams(
            dimension_semantics=("parallel", "parallel", "arbitrary"),
            vmem_limit_bytes=VMEM_LIMIT),
        name="ffn",
    )(h1, h1, h1, h1_meta, w_up, w_up, b_up, b_up, cw, cw, cb, cb, w_down, b_down, g2, b2)


TILE_M = 512
TILE_Q = 1024
TILE_K = 1024
TILE_FF = 512


def _trunk(x, meta_frame, p):
    bsz, t, d = x.shape
    tm, tq, tk = min(t, TILE_M), min(t, TILE_Q), min(t, TILE_K)
    assert t % tm == 0 and t % tq == 0 and t % tk == 0, (t, tm, tq, tk)
    q_m, k_m, v_m, c_m = meta_frame
    ident = lambda b: b
    zero = lambda b: 0

    q, k, v, c = _inproj(x, N_META, p["ge"], p["be"], p["w_in"], p["b_in"], tm=tm)
    att = _attention(q, k, v, k_m, v_m, p["lam"], p["subln_g"], bsz=bsz, tq=tq, tk=tk)
    att_m = _attention(q_m, k, v, k_m, v_m, p["lam"], p["subln_g"],
                       bsz=bsz, tq=META_ROWS, tk=tk)
    mix_w = (p["ge"], p["be"], p["conv_w"], p["conv_b"], p["conv_g"], p["conv_beta"],
             p["w_out"], p["b_out"], p["g1"], p["b1"])
    h1 = _mix(x, att, c, c, c_m, *mix_w, bsz=bsz, t=t, tm=tm,
              xb=ident, ab=ident, cmb=ident, zero_right_at_end=True)
    h1_m = _mix(p["meta"], att_m, c_m, c, jnp.zeros((1, HALO, D_ATT), BF16), *mix_w,
                bsz=bsz, t=N_META, tm=N_META, xb=zero, ab=ident, cmb=zero,
                zero_right_at_end=False)
    return _ffn(h1, h1_m, p["w_up"], p["b_up"], p["ffn_cw"], p["ffn_cb"], p["w_down"],
                p["b_down"], p["g2"], p["b2"], tm=tm, tn=TILE_FF)


def kernel(x_prompt, x_sample, meta_tokens, ln_emb_g, ln_emb_b, w_in, b_in, lambda_q1, lambda_k1,
           lambda_q2, lambda_k2, subln_g, conv_w, conv_b, conv_ln_g, conv_ln_b, w_out, b_out,
           ln1_g, ln1_b, w_up, b_up, ffn_conv_w, ffn_conv_b, w_down, b_down, ln2_g, ln2_b):
    l = 0
    row = lambda v: v.reshape(1, -1).astype(F32)
    p = dict(
        meta=meta_tokens[None].astype(F32),
        ge=row(ln_emb_g), be=row(ln_emb_b),
        w_in=w_in[l].astype(BF16), b_in=row(b_in[l]),
        lam=(row(lambda_q1[l]), row(lambda_k1[l]), row(lambda_q2[l]), row(lambda_k2[l])),
        subln_g=subln_g[l].reshape(-1, 1).astype(F32),
        conv_w=conv_w[l].astype(F32), conv_b=row(conv_b[l]),
        conv_g=row(conv_ln_g[l]), conv_beta=row(conv_ln_b[l]),
        w_out=w_out[l].astype(BF16), b_out=row(b_out[l]), g1=row(ln1_g[l]), b1=row(ln1_b[l]),
        w_up=w_up[l].astype(BF16), b_up=row(b_up[l]),
        ffn_cw=ffn_conv_w[l].astype(F32), ffn_cb=row(ffn_conv_b[l]),
        w_down=w_down[l].astype(BF16), b_down=row(b_down[l]), g2=row(ln2_g[l]), b2=row(ln2_b[l]),
    )
    d = meta_tokens.shape[-1]
    meta_pad = jnp.zeros((1, META_ROWS, d), F32).at[0, :N_META].set(meta_tokens.astype(F32))
    meta_frame = _inproj(meta_pad, 0, p["ge"], p["be"], p["w_in"], p["b_in"], tm=META_ROWS)
    return (_trunk(x_prompt, meta_frame, p), _trunk(x_sample, meta_frame, p))
```

```python
import functools
import math

import jax
import jax.numpy as jnp
from jax import lax
from jax.experimental import pallas as pl
from jax.experimental.pallas import tpu as pltpu

F32 = jnp.float32
BF16 = jnp.bfloat16

N_META = 16
N_HEADS = 8
HEAD_DIM = 64
V_DIM = 2 * HEAD_DIM
D_ATT = N_HEADS * V_DIM
D_QK = N_HEADS * 2 * HEAD_DIM
ROT_DIM = HEAD_DIM // 4
ROPE_THETA = 500000.0
CONV_WIDTH = 31
CONV_HALF = CONV_WIDTH // 2
FFN_CONV_WIDTH = 3
FFN_ROW_BIAS, FFN_ROW_TAP0, FFN_ROW_CBIAS = 0, 1, 1 + FFN_CONV_WIDTH
LN_EPS = 1e-5
DEPTH = 1
DEEPNORM_ALPHA = (2.0 * DEPTH) ** 0.25
LAMBDA_INIT = 0.8 - 0.6 * math.exp(-0.3 * 0)

LANES = 128
SUBLANES_F32 = 8
SUBLANES_BF16 = 16
META_ROWS = LANES
HALO = SUBLANES_BF16
MXU_WIDTH = 256
COL_CHUNK = MXU_WIDTH
ATTN_PARTS = 4
MIX_ROW_GROUP = 256
LOG2E = math.log2(math.e)
NEG_BIG = -1e30
VMEM_LIMIT = 56 * 1024 * 1024


def _ln(x, g, b):
    mu = jnp.mean(x, axis=-1, keepdims=True)
    xc = x - mu
    var = jnp.mean(xc * xc, axis=-1, keepdims=True)
    return xc * lax.rsqrt(var + LN_EPS) * g + b


def _sigmoid(x):
    return 1.0 / (1.0 + jnp.exp(-x))


def _inproj_kernel(x_ref, ge_ref, be_ref, w_ref, b_ref, cos_ref, sa_ref, sb_ref,
                   q_ref, k_ref, v_ref, c_ref, h0_sc, pa_sc, pb_sc):
    h0_sc[...] = _ln(x_ref[0], ge_ref[...], be_ref[...]).astype(BF16)

    def project(chunk, dst):
        cols = slice(chunk * D_QK, (chunk + 1) * D_QK)
        dst[...] = (jnp.dot(h0_sc[...], w_ref[:, cols], preferred_element_type=F32)
                    + b_ref[:, cols])

    def rope(y):
        cos, sa, sb = cos_ref[...], sa_ref[...], sb_ref[...]
        outs = []
        for blk in range(y.shape[1] // LANES):
            yb = y[:, blk * LANES:(blk + 1) * LANES]
            up = pltpu.roll(yb, LANES - ROT_DIM // 2, 1)
            dn = pltpu.roll(yb, ROT_DIM // 2, 1)
            outs.append(yb * cos + up * sa + dn * sb)
        return jnp.concatenate(outs, axis=1)

    project(0, pa_sc)
    project(1, pb_sc)
    q_ref[0] = (rope(pa_sc[...]) * (HEAD_DIM ** -0.5 * LOG2E)).astype(BF16)
    project(2, pa_sc)
    k_ref[0] = rope(pb_sc[...]).astype(BF16)
    project(3, pb_sc)
    v_ref[0] = pa_sc[...].astype(BF16)
    project(4, pa_sc)
    c_ref[0] = (pb_sc[...] * _sigmoid(pa_sc[...])).astype(BF16)


def _inproj(x, pos0, ge, be, w_in, b_in, *, tm):
    bsz, t, d = x.shape
    assert w_in.shape[1] == 5 * D_QK
    pos = (pos0 + jnp.arange(t, dtype=jnp.int32)).astype(F32)
    inv_freq = ROPE_THETA ** (-jnp.arange(0, ROT_DIM, 2, dtype=F32) / ROT_DIM)
    ang = pos[:, None] * inv_freq[None, :]
    cos, sin = jnp.cos(ang), jnp.sin(ang)
    half = ROT_DIM // 2
    one = jnp.ones((t, HEAD_DIM - ROT_DIM), F32)
    zero_r = jnp.zeros((t, HEAD_DIM - ROT_DIM), F32)
    zero_h = jnp.zeros((t, half), F32)
    cos64 = jnp.concatenate([cos, cos, one], axis=1)
    sa64 = jnp.concatenate([-sin, zero_h, zero_r], axis=1)
    sb64 = jnp.concatenate([zero_h, sin, zero_r], axis=1)
    rep = LANES // HEAD_DIM
    cos_t, sa_t, sb_t = (jnp.tile(a, (1, rep)) for a in (cos64, sa64, sb64))

    row_spec = pl.BlockSpec((1, tm, D_QK), lambda b, i: (b, i, 0))
    tab_spec = pl.BlockSpec((tm, LANES), lambda b, i: (i, 0))
    vec_d = pl.BlockSpec((1, d), lambda b, i: (0, 0))
    out = jax.ShapeDtypeStruct((bsz, t, D_QK), BF16)
    return pl.pallas_call(
        _inproj_kernel,
        grid=(bsz, t // tm),
        in_specs=[
            pl.BlockSpec((1, tm, d), lambda b, i: (b, i, 0)),
            vec_d, vec_d,
            pl.BlockSpec(w_in.shape, lambda b, i: (0, 0), pipeline_mode=pl.Buffered(1)),
            pl.BlockSpec(b_in.shape, lambda b, i: (0, 0)),
            tab_spec, tab_spec, tab_spec,
        ],
        out_specs=[row_spec, row_spec, row_spec, row_spec],
        out_shape=[out, out, out, out],
        scratch_shapes=[pltpu.VMEM((tm, d), BF16), pltpu.VMEM((tm, D_QK), F32),
                        pltpu.VMEM((tm, D_QK), F32)],
        compiler_params=pltpu.CompilerParams(
            dimension_semantics=("parallel", "parallel"),
            vmem_limit_bytes=VMEM_LIMIT),
        name="inproj",
    )(x, ge, be, w_in, b_in, cos_t, sa_t, sb_t)


def _attn_kernel(q_ref, k_ref, v_ref, km_ref, vm_ref, lq1_ref, lk1_ref, lq2_ref, lk2_ref,
                 g_ref, o_ref, qm_sc, m_sc, l_sc, acc_sc, s_sc, mb_sc, *, tq, tk, nkv):
    qT = q_ref[0].astype(F32).T
    feat = lax.broadcasted_iota(jnp.int32, qT.shape, 0)
    qm_sc[:, :tq] = jnp.where(feat < HEAD_DIM, qT, 0.0).astype(BF16)
    qm_sc[:, tq:] = jnp.where(feat >= HEAD_DIM, qT, 0.0).astype(BF16)

    n_part = min(ATTN_PARTS, (2 * tq) // LANES)
    part_w = (2 * tq) // n_part
    chunk = min(COL_CHUNK, part_w)

    def part_chunks(p):
        return [slice(p * part_w + c * chunk, p * part_w + (c + 1) * chunk)
                for c in range(part_w // chunk)]

    vTm = vm_ref[0].T
    for cols in [c for p in range(n_part) for c in part_chunks(p)]:
        s = jnp.dot(km_ref[0], qm_sc[:, cols], preferred_element_type=F32)
        key = lax.broadcasted_iota(jnp.int32, s.shape, 0)
        s = jnp.where(key < N_META, s, NEG_BIG)
        m = jnp.max(s, axis=0, keepdims=True)
        p = jnp.exp2(s - m)
        m_sc[:, cols] = m
        l_sc[:, cols] = jnp.sum(p, axis=0, keepdims=True)
        acc_sc[:, cols] = jnp.dot(vTm, p.astype(BF16), preferred_element_type=F32)

    def scores(j, p):
        k0 = pl.multiple_of(j * tk, tk)
        kb = k_ref[0, pl.ds(k0, tk), :]
        for cols in part_chunks(p):
            s = jnp.dot(kb, qm_sc[:, cols], preferred_element_type=F32)
            s_sc[:, cols] = s
            mb_sc[:, cols] = jnp.max(s, axis=0, keepdims=True)

    def absorb(j, p):
        k0 = pl.multiple_of(j * tk, tk)
        vb = v_ref[0, pl.ds(k0, tk), :].T
        for cols in part_chunks(p):
            m_old = m_sc[:, cols]
            m_new = jnp.maximum(m_old, mb_sc[:, cols])
            alpha = jnp.exp2(m_old - m_new)
            p_un = jnp.exp2(s_sc[:, cols] - m_new)
            l_sc[:, cols] = alpha * l_sc[:, cols] + jnp.sum(p_un, axis=0, keepdims=True)
            acc_sc[:, cols] = alpha * acc_sc[:, cols] + jnp.dot(
                vb, p_un.astype(BF16), preferred_element_type=F32)
            m_sc[:, cols] = m_new

    def key_block(j, last):
        for p in range(n_part):
            if p + 1 < n_part:
                scores(j, p + 1)
            elif not last:
                scores(j + 1, 0)
            absorb(j, p)

    scores(0, 0)

    def body(j, carry):
        key_block(j, last=False)
        return carry

    lax.fori_loop(0, nkv - 1, body, 0)
    key_block(nkv - 1, last=True)

    s1 = jnp.sum(lq1_ref[...] * lk1_ref[...], axis=1, keepdims=True)
    s2 = jnp.sum(lq2_ref[...] * lk2_ref[...], axis=1, keepdims=True)
    lam = jnp.exp(s1) - jnp.exp(s2) + LAMBDA_INIT
    o = acc_sc[...] * (1.0 / l_sc[...])
    o = o[:, :tq] - lam * o[:, tq:]
    ms = jnp.mean(o * o, axis=0, keepdims=True)
    on = o * lax.rsqrt(ms + LN_EPS) * g_ref[...] * (1.0 - LAMBDA_INIT)
    o_ref[0] = on.T.astype(BF16)


def _attention(q, k, v, k_meta, v_meta, lam_vecs, g_col, *, bsz, tq, tk):
    bq, t_q, _ = q.shape
    t = k.shape[1]
    nkv = t // tk
    qb = (lambda b: b) if bq == bsz else (lambda b: 0)
    vec = pl.BlockSpec((1, HEAD_DIM), lambda b, h, i: (0, 0))
    return pl.pallas_call(
        functools.partial(_attn_kernel, tq=tq, tk=tk, nkv=nkv),
        grid=(bsz, N_HEADS, t_q // tq),
        in_specs=[
            pl.BlockSpec((1, tq, V_DIM), lambda b, h, i: (qb(b), i, h)),
            pl.BlockSpec((1, t, V_DIM), lambda b, h, i: (b, 0, h)),
            pl.BlockSpec((1, t, V_DIM), lambda b, h, i: (b, 0, h)),
            pl.BlockSpec((1, META_ROWS, V_DIM), lambda b, h, i: (0, 0, h)),
            pl.BlockSpec((1, META_ROWS, V_DIM), lambda b, h, i: (0, 0, h)),
            vec, vec, vec, vec,
            pl.BlockSpec((V_DIM, 1), lambda b, h, i: (0, 0)),
        ],
        out_specs=pl.BlockSpec((1, tq, V_DIM), lambda b, h, i: (b, i, h)),
        out_shape=jax.ShapeDtypeStruct((bsz, t_q, D_ATT), BF16),
        scratch_shapes=[
            pltpu.VMEM((V_DIM, 2 * tq), BF16),
            pltpu.VMEM((1, 2 * tq), F32),
            pltpu.VMEM((1, 2 * tq), F32),
            pltpu.VMEM((V_DIM, 2 * tq), F32),
            pltpu.VMEM((tk, 2 * tq), F32),
            pltpu.VMEM((1, 2 * tq), F32),
        ],
        compiler_params=pltpu.CompilerParams(
            dimension_semantics=("parallel", "parallel", "parallel"),
            vmem_limit_bytes=VMEM_LIMIT),
        name="attention",
    )(q, k, v, k_meta, v_meta, *lam_vecs, g_col)


def _mix_kernel(x_ref, ge_ref, be_ref, att_ref, cmain_ref, cprev_ref, cnext_ref, cfirst_ref,
                cw_ref, cb_ref, cg_ref, cbeta_ref, wo_ref, bo_ref, g1_ref, b1_ref,
                h1_ref, win_sc, conv_sc, mix_sc, *, tm, n_tiles, zero_right_at_end, row_chunk):
    i = pl.program_id(1)
    left = jnp.where(i == 0, cfirst_ref[0].astype(F32), cprev_ref[0].astype(F32))
    right = cnext_ref[0].astype(F32)
    if zero_right_at_end:
        right = jnp.where(i == n_tiles - 1, 0.0, right)
    base = HALO - CONV_HALF
    n_lane = conv_sc.shape[1] // LANES
    for lc in range(n_lane):
        lanes = slice(lc * LANES, (lc + 1) * LANES)
        win_sc[lc, 0:HALO, :] = left[:, lanes]
        win_sc[lc, HALO:HALO + tm, :] = cmain_ref[0, :, lanes].astype(F32)
        win_sc[lc, HALO + tm:, :] = right[:, lanes]

    mix_sc[...] = (jnp.dot(att_ref[0], wo_ref[0:D_ATT, :], preferred_element_type=F32)
                   + bo_ref[...])
    row_group = min(tm, MIX_ROW_GROUP)
    for g in range(tm // row_group):
        g0 = g * row_group
        for r0 in range(g0, g0 + row_group, row_chunk):
            for lc in range(n_lane):
                lanes = slice(lc * LANES, (lc + 1) * LANES)
                acc = jnp.zeros((row_chunk, LANES), F32) + cb_ref[:, lanes]
                for t in range(CONV_WIDTH):
                    acc = acc + (cw_ref[t:t + 1, lanes]
                                 * win_sc[lc, r0 + base + t:r0 + base + t + row_chunk, :])
                conv_sc[r0:r0 + row_chunk, lanes] = acc
        rows = slice(g0, g0 + row_group)
        y = _ln(conv_sc[rows, :], cg_ref[...], cbeta_ref[...])
        cc = (y * _sigmoid(y)).astype(BF16)
        mix_sc[rows, :] += jnp.dot(cc, wo_ref[D_ATT:, :], preferred_element_type=F32)
    h0 = _ln(x_ref[0], ge_ref[...], be_ref[...])
    h1_ref[0] = _ln(DEEPNORM_ALPHA * h0 + mix_sc[...], g1_ref[...], b1_ref[...])


def _mix(x, att, c_main, c_halo, c_first, ge, be, conv_w, conv_b, conv_g, conv_beta,
         w_out, b_out, g1, b1, *, bsz, t, tm, xb, ab, cmb, zero_right_at_end):
    d = x.shape[-1]
    n_tiles = t // tm
    hb = tm // HALO
    n_hblk = c_halo.shape[1] // HALO
    row_chunk = min(tm, 64)
    vec_c = pl.BlockSpec((1, D_ATT), lambda b, i: (0, 0))
    vec_d = pl.BlockSpec((1, d), lambda b, i: (0, 0))
    return pl.pallas_call(
        functools.partial(_mix_kernel, tm=tm, n_tiles=n_tiles,
                          zero_right_at_end=zero_right_at_end, row_chunk=row_chunk),
        grid=(bsz, n_tiles),
        in_specs=[
            pl.BlockSpec((1, tm, d), lambda b, i: (xb(b), i, 0)),
            vec_d, vec_d,
            pl.BlockSpec((1, tm, D_ATT), lambda b, i: (ab(b), i, 0)),
            pl.BlockSpec((1, tm, D_ATT), lambda b, i: (cmb(b), i, 0)),
            pl.BlockSpec((1, HALO, D_ATT), lambda b, i: (b, jnp.maximum(i * hb - 1, 0), 0)),
            pl.BlockSpec((1, HALO, D_ATT),
                         lambda b, i: (b, jnp.minimum((i + 1) * hb, n_hblk - 1), 0))
            if zero_right_at_end else
            pl.BlockSpec((1, HALO, D_ATT), lambda b, i: (b, 0, 0)),
            pl.BlockSpec((1, HALO, D_ATT), lambda b, i: (0, 0, 0)),
            pl.BlockSpec((CONV_WIDTH, D_ATT), lambda b, i: (0, 0)),
            vec_c, vec_c, vec_c,
            pl.BlockSpec((d, d), lambda b, i: (0, 0)),
            vec_d, vec_d, vec_d,
        ],
        out_specs=pl.BlockSpec((1, tm, d), lambda b, i: (b, i, 0)),
        out_shape=jax.ShapeDtypeStruct((bsz, t, d), F32),
        scratch_shapes=[pltpu.VMEM((D_ATT // LANES, tm + 2 * HALO, LANES), F32),
                        pltpu.VMEM((tm, D_ATT), F32), pltpu.VMEM((tm, d), F32)],
        compiler_params=pltpu.CompilerParams(
            dimension_semantics=("parallel", "parallel"),
            vmem_limit_bytes=VMEM_LIMIT),
        name="mix",
    )(x, ge, be, att, c_main, c_halo, c_halo, c_first, conv_w, conv_b, conv_g, conv_beta,
      w_out, b_out, g1, b1)


def _ffn_kernel(h_ref, hprev_ref, hnext_ref, hmeta_ref, wu_ref, pu_ref, wd_ref, bd_ref,
                g2_ref, b2_ref, y_ref, hb_sc, acc_sc, u_sc, *, tm, tn, n_tiles, nj):
    i = pl.program_id(1)
    j = pl.program_id(2)
    pad = SUBLANES_BF16
    d = h_ref.shape[-1]

    @pl.when(j == 0)
    def _():
        lastrow = SUBLANES_F32 - 1
        prev_row = jnp.where(i == 0, hmeta_ref[0, lastrow:lastrow + 1, :],
                             hprev_ref[0, lastrow:lastrow + 1, :])
        next_row = hnext_ref[0, 0:1, :]
        r = lax.broadcasted_iota(jnp.int32, (pad, d), 0)
        hb_sc[0:pad, :] = jnp.where(r == pad - 1, prev_row, 0.0).astype(BF16)
        hb_sc[pad:pad + tm, :] = h_ref[0].astype(BF16)
        hb_sc[pad + tm:, :] = jnp.where(r == 0, next_row, 0.0).astype(BF16)
        acc_sc[...] = jnp.zeros_like(acc_sc)

    u_sc[...] = (jnp.dot(hb_sc[...], wu_ref[0], preferred_element_type=F32)
                 + pu_ref[0, FFN_ROW_BIAS:FFN_ROW_BIAS + 1, :])
    keep = jnp.where(i == n_tiles - 1, 0.0, 1.0)
    tail = slice(pad + tm, pad + tm + SUBLANES_F32)
    u_sc[tail, :] = u_sc[tail, :] * keep

    def conv3(cols):
        taps = [pu_ref[0, FFN_ROW_TAP0 + t:FFN_ROW_TAP0 + t + 1, cols]
                for t in range(FFN_CONV_WIDTH)]
        return (taps[0] * u_sc[pad - 1:pad - 1 + tm, cols]
                + taps[1] * u_sc[pad:pad + tm, cols]
                + taps[2] * u_sc[pad + 1:pad + 1 + tm, cols]
                + pu_ref[0, FFN_ROW_CBIAS:FFN_ROW_CBIAS + 1, cols])

    parts = []
    for c in range(tn // MXU_WIDTH):
        gate = conv3(slice(c * MXU_WIDTH, (c + 1) * MXU_WIDTH))
        val = conv3(slice(tn + c * MXU_WIDTH, tn + (c + 1) * MXU_WIDTH))
        f = (gate * _sigmoid(gate) * val).astype(BF16)
        parts.append(jnp.dot(f, wd_ref[c * MXU_WIDTH:(c + 1) * MXU_WIDTH, :],
                             preferred_element_type=F32))
    acc_sc[...] += functools.reduce(lambda a, b: a + b, parts)

    @pl.when(j == nj - 1)
    def _():
        y_ref[0] = _ln(DEEPNORM_ALPHA * h_ref[0] + acc_sc[...] + bd_ref[...],
                       g2_ref[...], b2_ref[...])


def _ffn_pack(w_up, b_up, cw, cb, tn):
    d, two_ff = w_up.shape
    nj = two_ff // (2 * tn)

    def chunked(a):
        r = a.shape[0]
        return a.reshape(r, 2, nj, tn).transpose(2, 0, 1, 3).reshape(nj, r, 2 * tn)

    rows = jnp.concatenate(
        [b_up.reshape(1, -1), cw, cb.reshape(1, -1),
         jnp.zeros((SUBLANES_F32 - FFN_ROW_CBIAS - 1, two_ff), F32)], axis=0).astype(F32)
    return chunked(w_up).astype(BF16), chunked(rows)


def _ffn(h1, h1_meta, wu, pu, w_down, b_down, g2, b2, *, tm):
    bsz, t, d = h1.shape
    nj, _, tn2 = wu.shape
    tn = tn2 // 2
    n_tiles = t // tm
    hb = tm // SUBLANES_F32
    n_hblk = t // SUBLANES_F32
    vec_d = pl.BlockSpec((1, d), lambda b, i, j: (0, 0))
    return pl.pallas_call(
        functools.partial(_ffn_kernel, tm=tm, tn=tn, n_tiles=n_tiles, nj=nj),
        grid=(bsz, n_tiles, nj),
        in_specs=[
            pl.BlockSpec((1, tm, d), lambda b, i, j: (b, i, 0)),
            pl.BlockSpec((1, SUBLANES_F32, d), lambda b, i, j: (b, jnp.maximum(i * hb - 1, 0), 0)),
            pl.BlockSpec((1, SUBLANES_F32, d),
                         lambda b, i, j: (b, jnp.minimum((i + 1) * hb, n_hblk - 1), 0)),
            pl.BlockSpec((1, SUBLANES_F32, d), lambda b, i, j: (b, N_META // SUBLANES_F32 - 1, 0)),
            pl.BlockSpec((1, d, tn2), lambda b, i, j: (j, 0, 0)),
            pl.BlockSpec((1, SUBLANES_F32, tn2), lambda b, i, j: (j, 0, 0)),
            pl.BlockSpec((tn, d), lambda b, i, j: (j, 0)),
            vec_d, vec_d, vec_d,
        ],
        out_specs=pl.BlockSpec((1, tm, d), lambda b, i, j: (b, i, 0)),
        out_shape=jax.ShapeDtypeStruct((bsz, t, d), F32),
        scratch_shapes=[pltpu.VMEM((tm + 2 * SUBLANES_BF16, d), BF16), pltpu.VMEM((tm, d), F32),
                        pltpu.VMEM((tm + 2 * SUBLANES_BF16, tn2), F32)],
        compiler_params=pltpu.CompilerParams(
            dimension_semantics=("parallel", "parallel", "arbitrary"),
            vmem_limit_bytes=VMEM_LIMIT),
        name="ffn",
    )(h1, h1, h1, h1_meta, wu, pu, w_down, b_down, g2, b2)


TILE_M = 512
TILE_Q = 1024
TILE_K = 1024
TILE_FF = 512


def _trunk(x, meta_frame, p):
    bsz, t, d = x.shape
    tm, tq, tk = min(t, TILE_M), min(t, TILE_Q), min(t, TILE_K)
    assert t % tm == 0 and t % tq == 0 and t % tk == 0, (t, tm, tq, tk)
    q_m, k_m, v_m, c_m = meta_frame
    ident = lambda b: b
    zero = lambda b: 0

    q, k, v, c = _inproj(x, N_META, p["ge"], p["be"], p["w_in"], p["b_in"], tm=tm)
    att = _attention(q, k, v, k_m, v_m, p["lam"], p["subln_g"], bsz=bsz, tq=tq, tk=tk)
    att_m = _attention(q_m, k, v, k_m, v_m, p["lam"], p["subln_g"],
                       bsz=bsz, tq=META_ROWS, tk=tk)
    mix_w = (p["ge"], p["be"], p["conv_w"], p["conv_b"], p["conv_g"], p["conv_beta"],
             p["w_out"], p["b_out"], p["g1"], p["b1"])
    h1 = _mix(x, att, c, c, c_m, *mix_w, bsz=bsz, t=t, tm=tm,
              xb=ident, ab=ident, cmb=ident, zero_right_at_end=True)
    h1_m = _mix(p["meta"], att_m, c_m, c, jnp.zeros((1, HALO, D_ATT), BF16), *mix_w,
                bsz=bsz, t=N_META, tm=N_META, xb=zero, ab=ident, cmb=zero,
                zero_right_at_end=False)
    return _ffn(h1, h1_m, p["w_up_chunks"], p["ffn_cols"], p["w_down"],
                p["b_down"], p["g2"], p["b2"], tm=tm)


def kernel(x_prompt, x_sample, meta_tokens, ln_emb_g, ln_emb_b, w_in, b_in, lambda_q1, lambda_k1,
           lambda_q2, lambda_k2, subln_g, conv_w, conv_b, conv_ln_g, conv_ln_b, w_out, b_out,
           ln1_g, ln1_b, w_up, b_up, ffn_conv_w, ffn_conv_b, w_down, b_down, ln2_g, ln2_b):
    l = 0
    row = lambda v: v.reshape(1, -1).astype(F32)
    w_up_chunks, ffn_cols = _ffn_pack(w_up[l], b_up[l], ffn_conv_w[l], ffn_conv_b[l], TILE_FF)
    p = dict(
        meta=meta_tokens[None].astype(F32),
        ge=row(ln_emb_g), be=row(ln_emb_b),
        w_in=w_in[l].astype(BF16), b_in=row(b_in[l]),
        lam=(row(lambda_q1[l]), row(lambda_k1[l]), row(lambda_q2[l]), row(lambda_k2[l])),
        subln_g=subln_g[l].reshape(-1, 1).astype(F32),
        conv_w=conv_w[l].astype(F32), conv_b=row(conv_b[l]),
        conv_g=row(conv_ln_g[l]), conv_beta=row(conv_ln_b[l]),
        w_out=w_out[l].astype(BF16), b_out=row(b_out[l]), g1=row(ln1_g[l]), b1=row(ln1_b[l]),
        w_up_chunks=w_up_chunks, ffn_cols=ffn_cols,
        w_down=w_down[l].astype(BF16), b_down=row(b_down[l]), g2=row(ln2_g[l]), b2=row(ln2_b[l]),
    )
    d = meta_tokens.shape[-1]
    meta_pad = jnp.zeros((1, META_ROWS, d), F32).at[0, :N_META].set(meta_tokens.astype(F32))
    meta_frame = _inproj(meta_pad, 0, p["ge"], p["be"], p["w_in"], p["b_in"], tm=META_ROWS)
    return (_trunk(x_prompt, meta_frame, p), _trunk(x_sample, meta_frame, p))
```

```python
import functools
import math

import jax
import jax.numpy as jnp
from jax import lax
from jax.experimental import pallas as pl
from jax.experimental.pallas import tpu as pltpu

F32 = jnp.float32
BF16 = jnp.bfloat16

N_META = 16
N_HEADS = 8
HEAD_DIM = 64
V_DIM = 2 * HEAD_DIM
D_ATT = N_HEADS * V_DIM
D_QK = N_HEADS * 2 * HEAD_DIM
ROT_DIM = HEAD_DIM // 4
ROPE_THETA = 500000.0
CONV_WIDTH = 31
CONV_HALF = CONV_WIDTH // 2
FFN_CONV_WIDTH = 3
LN_EPS = 1e-5
DEPTH = 1
DEEPNORM_ALPHA = (2.0 * DEPTH) ** 0.25
LAMBDA_INIT = 0.8 - 0.6 * math.exp(-0.3 * 0)

LANES = 128
SUBLANES_F32 = 8
SUBLANES_BF16 = 16
META_ROWS = LANES
HALO = SUBLANES_BF16
MXU_WIDTH = 256
COL_CHUNK = MXU_WIDTH
MIX_ROW_GROUP = 256
LOG2E = math.log2(math.e)
NEG_BIG = -1e30
VMEM_LIMIT = 56 * 1024 * 1024


def _ln(x, g, b):
    mu = jnp.mean(x, axis=-1, keepdims=True)
    xc = x - mu
    var = jnp.mean(xc * xc, axis=-1, keepdims=True)
    return xc * lax.rsqrt(var + LN_EPS) * g + b


def _sigmoid(x):
    return 1.0 / (1.0 + jnp.exp(-x))


def _inproj_kernel(x_ref, ge_ref, be_ref, w_ref, b_ref, cos_ref, sa_ref, sb_ref,
                   q_ref, k_ref, v_ref, c_ref, h0_sc, pa_sc, pb_sc):
    h0_sc[...] = _ln(x_ref[0], ge_ref[...], be_ref[...]).astype(BF16)

    def project(chunk, dst):
        cols = slice(chunk * D_QK, (chunk + 1) * D_QK)
        dst[...] = (jnp.dot(h0_sc[...], w_ref[:, cols], preferred_element_type=F32)
                    + b_ref[:, cols])

    def rope(y):
        cos, sa, sb = cos_ref[...], sa_ref[...], sb_ref[...]
        outs = []
        for blk in range(y.shape[1] // LANES):
            yb = y[:, blk * LANES:(blk + 1) * LANES]
            up = pltpu.roll(yb, LANES - ROT_DIM // 2, 1)
            dn = pltpu.roll(yb, ROT_DIM // 2, 1)
            outs.append(yb * cos + up * sa + dn * sb)
        return jnp.concatenate(outs, axis=1)

    project(0, pa_sc)
    project(1, pb_sc)
    q_ref[0] = (rope(pa_sc[...]) * (HEAD_DIM ** -0.5 * LOG2E)).astype(BF16)
    project(2, pa_sc)
    k_ref[0] = rope(pb_sc[...]).astype(BF16)
    project(3, pb_sc)
    v_ref[0] = pa_sc[...].astype(BF16)
    project(4, pa_sc)
    c_ref[0] = (pb_sc[...] * _sigmoid(pa_sc[...])).astype(BF16)


def _inproj(x, pos0, ge, be, w_in, b_in, *, tm):
    bsz, t, d = x.shape
    assert w_in.shape[1] == 5 * D_QK
    pos = (pos0 + jnp.arange(t, dtype=jnp.int32)).astype(F32)
    inv_freq = ROPE_THETA ** (-jnp.arange(0, ROT_DIM, 2, dtype=F32) / ROT_DIM)
    ang = pos[:, None] * inv_freq[None, :]
    cos, sin = jnp.cos(ang), jnp.sin(ang)
    half = ROT_DIM // 2
    one = jnp.ones((t, HEAD_DIM - ROT_DIM), F32)
    zero_r = jnp.zeros((t, HEAD_DIM - ROT_DIM), F32)
    zero_h = jnp.zeros((t, half), F32)
    cos64 = jnp.concatenate([cos, cos, one], axis=1)
    sa64 = jnp.concatenate([-sin, zero_h, zero_r], axis=1)
    sb64 = jnp.concatenate([zero_h, sin, zero_r], axis=1)
    rep = LANES // HEAD_DIM
    cos_t, sa_t, sb_t = (jnp.tile(a, (1, rep)) for a in (cos64, sa64, sb64))

    row_spec = pl.BlockSpec((1, tm, D_QK), lambda b, i: (b, i, 0))
    tab_spec = pl.BlockSpec((tm, LANES), lambda b, i: (i, 0))
    vec_d = pl.BlockSpec((1, d), lambda b, i: (0, 0))
    out = jax.ShapeDtypeStruct((bsz, t, D_QK), BF16)
    return pl.pallas_call(
        _inproj_kernel,
        grid=(bsz, t // tm),
        in_specs=[
            pl.BlockSpec((1, tm, d), lambda b, i: (b, i, 0)),
            vec_d, vec_d,
            pl.BlockSpec(w_in.shape, lambda b, i: (0, 0), pipeline_mode=pl.Buffered(1)),
            pl.BlockSpec(b_in.shape, lambda b, i: (0, 0)),
            tab_spec, tab_spec, tab_spec,
        ],
        out_specs=[row_spec, row_spec, row_spec, row_spec],
        out_shape=[out, out, out, out],
        scratch_shapes=[pltpu.VMEM((tm, d), BF16), pltpu.VMEM((tm, D_QK), F32),
                        pltpu.VMEM((tm, D_QK), F32)],
        compiler_params=pltpu.CompilerParams(
            dimension_semantics=("parallel", "parallel"),
            vmem_limit_bytes=VMEM_LIMIT),
        name="inproj",
    )(x, ge, be, w_in, b_in, cos_t, sa_t, sb_t)


def _attn_kernel(q_ref, k_ref, v_ref, km_ref, vm_ref, lq1_ref, lk1_ref, lq2_ref, lk2_ref,
                 g_ref, o_ref, qm_sc, m_sc, l_sc, acc_sc, sa_sc, sb_sc, mba_sc, mbb_sc,
                 *, tq, tk, nkv):
    qT = q_ref[0].astype(F32).T
    feat = lax.broadcasted_iota(jnp.int32, qT.shape, 0)
    qm_sc[:, :tq] = jnp.where(feat < HEAD_DIM, qT, 0.0).astype(BF16)
    qm_sc[:, tq:] = jnp.where(feat >= HEAD_DIM, qT, 0.0).astype(BF16)

    n_col = (2 * tq) // COL_CHUNK
    vTm = vm_ref[0].T
    for c in range(n_col):
        cols = slice(c * COL_CHUNK, (c + 1) * COL_CHUNK)
        s = jnp.dot(km_ref[0], qm_sc[:, cols], preferred_element_type=F32)
        key = lax.broadcasted_iota(jnp.int32, s.shape, 0)
        s = jnp.where(key < N_META, s, NEG_BIG)
        m = jnp.max(s, axis=0, keepdims=True)
        p = jnp.exp2(s - m)
        m_sc[:, cols] = m
        l_sc[:, cols] = jnp.sum(p, axis=0, keepdims=True)
        acc_sc[:, cols] = jnp.dot(vTm, p.astype(BF16), preferred_element_type=F32)

    def scores(j, s_ref, mb_ref):
        k0 = pl.multiple_of(j * tk, tk)
        kb = k_ref[0, pl.ds(k0, tk), :]
        for c in range(n_col):
            cols = slice(c * COL_CHUNK, (c + 1) * COL_CHUNK)
            s = jnp.dot(kb, qm_sc[:, cols], preferred_element_type=F32)
            s_ref[:, cols] = s
            mb_ref[:, cols] = jnp.max(s, axis=0, keepdims=True)

    def absorb(j, s_ref, mb_ref):
        k0 = pl.multiple_of(j * tk, tk)
        vb = v_ref[0, pl.ds(k0, tk), :].T
        for c in range(n_col):
            cols = slice(c * COL_CHUNK, (c + 1) * COL_CHUNK)
            m_old = m_sc[:, cols]
            m_new = jnp.maximum(m_old, mb_ref[:, cols])
            alpha = jnp.exp2(m_old - m_new)
            p = jnp.exp2(s_ref[:, cols] - m_new)
            l_sc[:, cols] = alpha * l_sc[:, cols] + jnp.sum(p, axis=0, keepdims=True)
            acc_sc[:, cols] = alpha * acc_sc[:, cols] + jnp.dot(
                vb, p.astype(BF16), preferred_element_type=F32)
            m_sc[:, cols] = m_new

    scores(0, sa_sc, mba_sc)

    def kv_pair(jj, carry):
        j = 2 * jj
        scores(j + 1, sb_sc, mbb_sc)
        absorb(j, sa_sc, mba_sc)
        scores(j + 2, sa_sc, mba_sc)
        absorb(j + 1, sb_sc, mbb_sc)
        return carry

    lax.fori_loop(0, nkv // 2 - 1, kv_pair, 0)
    scores(nkv - 1, sb_sc, mbb_sc)
    absorb(nkv - 2, sa_sc, mba_sc)
    absorb(nkv - 1, sb_sc, mbb_sc)

    s1 = jnp.sum(lq1_ref[...] * lk1_ref[...], axis=1, keepdims=True)
    s2 = jnp.sum(lq2_ref[...] * lk2_ref[...], axis=1, keepdims=True)
    lam = jnp.exp(s1) - jnp.exp(s2) + LAMBDA_INIT
    o = acc_sc[...] * (1.0 / l_sc[...])
    o = o[:, :tq] - lam * o[:, tq:]
    ms = jnp.mean(o * o, axis=0, keepdims=True)
    on = o * lax.rsqrt(ms + LN_EPS) * g_ref[...] * (1.0 - LAMBDA_INIT)
    o_ref[0] = on.T.astype(BF16)


def _attention(q, k, v, k_meta, v_meta, lam_vecs, g_col, *, bsz, tq, tk):
    bq, t_q, _ = q.shape
    t = k.shape[1]
    nkv = t // tk
    assert nkv >= 2 and nkv % 2 == 0, "the key loop is pipelined over pairs of key blocks"
    qb = (lambda b: b) if bq == bsz else (lambda b: 0)
    vec = pl.BlockSpec((1, HEAD_DIM), lambda b, h, i: (0, 0))
    return pl.pallas_call(
        functools.partial(_attn_kernel, tq=tq, tk=tk, nkv=nkv),
        grid=(bsz, N_HEADS, t_q // tq),
        in_specs=[
            pl.BlockSpec((1, tq, V_DIM), lambda b, h, i: (qb(b), i, h)),
            pl.BlockSpec((1, t, V_DIM), lambda b, h, i: (b, 0, h)),
            pl.BlockSpec((1, t, V_DIM), lambda b, h, i: (b, 0, h)),
            pl.BlockSpec((1, META_ROWS, V_DIM), lambda b, h, i: (0, 0, h)),
            pl.BlockSpec((1, META_ROWS, V_DIM), lambda b, h, i: (0, 0, h)),
            vec, vec, vec, vec,
            pl.BlockSpec((V_DIM, 1), lambda b, h, i: (0, 0)),
        ],
        out_specs=pl.BlockSpec((1, tq, V_DIM), lambda b, h, i: (b, i, h)),
        out_shape=jax.ShapeDtypeStruct((bsz, t_q, D_ATT), BF16),
        scratch_shapes=[
            pltpu.VMEM((V_DIM, 2 * tq), BF16),
            pltpu.VMEM((1, 2 * tq), F32),
            pltpu.VMEM((1, 2 * tq), F32),
            pltpu.VMEM((V_DIM, 2 * tq), F32),
            pltpu.VMEM((tk, 2 * tq), F32),
            pltpu.VMEM((tk, 2 * tq), F32),
            pltpu.VMEM((1, 2 * tq), F32),
            pltpu.VMEM((1, 2 * tq), F32),
        ],
        compiler_params=pltpu.CompilerParams(
            dimension_semantics=("parallel", "parallel", "parallel"),
            vmem_limit_bytes=VMEM_LIMIT),
        name="attention",
    )(q, k, v, k_meta, v_meta, *lam_vecs, g_col)


def _mix_kernel(x_ref, ge_ref, be_ref, att_ref, cmain_ref, cprev_ref, cnext_ref, cfirst_ref,
                cw_ref, cb_ref, cg_ref, cbeta_ref, wo_ref, bo_ref, g1_ref, b1_ref,
                h1_ref, win_sc, conv_sc, mix_sc, *, tm, n_tiles, zero_right_at_end, row_chunk):
    i = pl.program_id(1)
    left = jnp.where(i == 0, cfirst_ref[0].astype(F32), cprev_ref[0].astype(F32))
    right = cnext_ref[0].astype(F32)
    if zero_right_at_end:
        right = jnp.where(i == n_tiles - 1, 0.0, right)
    base = HALO - CONV_HALF
    n_lane = conv_sc.shape[1] // LANES
    for lc in range(n_lane):
        lanes = slice(lc * LANES, (lc + 1) * LANES)
        win_sc[lc, 0:HALO, :] = left[:, lanes]
        win_sc[lc, HALO:HALO + tm, :] = cmain_ref[0, :, lanes].astype(F32)
        win_sc[lc, HALO + tm:, :] = right[:, lanes]

    mix_sc[...] = (jnp.dot(att_ref[0], wo_ref[0:D_ATT, :], preferred_element_type=F32)
                   + bo_ref[...])
    row_group = min(tm, MIX_ROW_GROUP)
    for g in range(tm // row_group):
        g0 = g * row_group
        for r0 in range(g0, g0 + row_group, row_chunk):
            for lc in range(n_lane):
                lanes = slice(lc * LANES, (lc + 1) * LANES)
                acc = jnp.zeros((row_chunk, LANES), F32) + cb_ref[:, lanes]
                for t in range(CONV_WIDTH):
                    acc = acc + (cw_ref[t:t + 1, lanes]
                                 * win_sc[lc, r0 + base + t:r0 + base + t + row_chunk, :])
                conv_sc[r0:r0 + row_chunk, lanes] = acc
        rows = slice(g0, g0 + row_group)
        y = _ln(conv_sc[rows, :], cg_ref[...], cbeta_ref[...])
        cc = (y * _sigmoid(y)).astype(BF16)
        mix_sc[rows, :] += jnp.dot(cc, wo_ref[D_ATT:, :], preferred_element_type=F32)
    h0 = _ln(x_ref[0], ge_ref[...], be_ref[...])
    h1_ref[0] = _ln(DEEPNORM_ALPHA * h0 + mix_sc[...], g1_ref[...], b1_ref[...])


def _mix(x, att, c_main, c_halo, c_first, ge, be, conv_w, conv_b, conv_g, conv_beta,
         w_out, b_out, g1, b1, *, bsz, t, tm, xb, ab, cmb, zero_right_at_end):
    d = x.shape[-1]
    n_tiles = t // tm
    hb = tm // HALO
    n_hblk = c_halo.shape[1] // HALO
    row_chunk = min(tm, 64)
    vec_c = pl.BlockSpec((1, D_ATT), lambda b, i: (0, 0))
    vec_d = pl.BlockSpec((1, d), lambda b, i: (0, 0))
    return pl.pallas_call(
        functools.partial(_mix_kernel, tm=tm, n_tiles=n_tiles,
                          zero_right_at_end=zero_right_at_end, row_chunk=row_chunk),
        grid=(bsz, n_tiles),
        in_specs=[
            pl.BlockSpec((1, tm, d), lambda b, i: (xb(b), i, 0)),
            vec_d, vec_d,
            pl.BlockSpec((1, tm, D_ATT), lambda b, i: (ab(b), i, 0)),
            pl.BlockSpec((1, tm, D_ATT), lambda b, i: (cmb(b), i, 0)),
            pl.BlockSpec((1, HALO, D_ATT), lambda b, i: (b, jnp.maximum(i * hb - 1, 0), 0)),
            pl.BlockSpec((1, HALO, D_ATT),
                         lambda b, i: (b, jnp.minimum((i + 1) * hb, n_hblk - 1), 0))
            if zero_right_at_end else
            pl.BlockSpec((1, HALO, D_ATT), lambda b, i: (b, 0, 0)),
            pl.BlockSpec((1, HALO, D_ATT), lambda b, i: (0, 0, 0)),
            pl.BlockSpec((CONV_WIDTH, D_ATT), lambda b, i: (0, 0)),
            vec_c, vec_c, vec_c,
            pl.BlockSpec((d, d), lambda b, i: (0, 0)),
            vec_d, vec_d, vec_d,
        ],
        out_specs=pl.BlockSpec((1, tm, d), lambda b, i: (b, i, 0)),
        out_shape=jax.ShapeDtypeStruct((bsz, t, d), F32),
        scratch_shapes=[pltpu.VMEM((D_ATT // LANES, tm + 2 * HALO, LANES), F32),
                        pltpu.VMEM((tm, D_ATT), F32), pltpu.VMEM((tm, d), F32)],
        compiler_params=pltpu.CompilerParams(
            dimension_semantics=("parallel", "parallel"),
            vmem_limit_bytes=VMEM_LIMIT),
        name="mix",
    )(x, ge, be, att, c_main, c_halo, c_halo, c_first, conv_w, conv_b, conv_g, conv_beta,
      w_out, b_out, g1, b1)


def _ffn_kernel(h_ref, hprev_ref, hnext_ref, hmeta_ref, wg_ref, wv_ref, bg_ref, bv_ref,
                cwg_ref, cwv_ref, cbg_ref, cbv_ref, wd_ref, bd_ref, g2_ref, b2_ref,
                y_ref, hb_sc, acc_sc, ug_sc, uv_sc, *, tm, n_tiles, nj):
    i = pl.program_id(1)
    j = pl.program_id(2)
    pad = SUBLANES_BF16
    d = h_ref.shape[-1]

    @pl.when(j == 0)
    def _():
        lastrow = SUBLANES_F32 - 1
        prev_row = jnp.where(i == 0, hmeta_ref[0, lastrow:lastrow + 1, :],
                             hprev_ref[0, lastrow:lastrow + 1, :])
        next_row = hnext_ref[0, 0:1, :]
        r = lax.broadcasted_iota(jnp.int32, (pad, d), 0)
        hb_sc[0:pad, :] = jnp.where(r == pad - 1, prev_row, 0.0).astype(BF16)
        hb_sc[pad:pad + tm, :] = h_ref[0].astype(BF16)
        hb_sc[pad + tm:, :] = jnp.where(r == 0, next_row, 0.0).astype(BF16)
        acc_sc[...] = jnp.zeros_like(acc_sc)

    hb = hb_sc[...]
    ug_sc[...] = jnp.dot(hb, wg_ref[...], preferred_element_type=F32) + bg_ref[...]
    uv_sc[...] = jnp.dot(hb, wv_ref[...], preferred_element_type=F32) + bv_ref[...]
    keep = jnp.where(i == n_tiles - 1, 0.0, 1.0)
    tail = slice(pad + tm, pad + tm + SUBLANES_F32)
    ug_sc[tail, :] = ug_sc[tail, :] * keep
    uv_sc[tail, :] = uv_sc[tail, :] * keep

    def conv3(u_sc, cw_ref, cb_ref, cols):
        return (cw_ref[0:1, cols] * u_sc[pad - 1:pad - 1 + tm, cols]
                + cw_ref[1:2, cols] * u_sc[pad:pad + tm, cols]
                + cw_ref[2:3, cols] * u_sc[pad + 1:pad + 1 + tm, cols] + cb_ref[:, cols])

    tn = wg_ref.shape[1]
    parts = []
    for c in range(tn // MXU_WIDTH):
        cols = slice(c * MXU_WIDTH, (c + 1) * MXU_WIDTH)
        gate = conv3(ug_sc, cwg_ref, cbg_ref, cols)
        val = conv3(uv_sc, cwv_ref, cbv_ref, cols)
        f = (gate * _sigmoid(gate) * val).astype(BF16)
        parts.append(jnp.dot(f, wd_ref[cols, :], preferred_element_type=F32))
    acc_sc[...] += functools.reduce(lambda a, b: a + b, parts)

    @pl.when(j == nj - 1)
    def _():
        y_ref[0] = _ln(DEEPNORM_ALPHA * h_ref[0] + acc_sc[...] + bd_ref[...],
                       g2_ref[...], b2_ref[...])


def _ffn(h1, h1_meta, w_up, b_up, cw, cb, w_down, b_down, g2, b2, *, tm, tn):
    bsz, t, d = h1.shape
    d_ff = w_down.shape[0]
    nj = d_ff // tn
    n_tiles = t // tm
    hb = tm // SUBLANES_F32
    n_hblk = t // SUBLANES_F32
    vec_d = pl.BlockSpec((1, d), lambda b, i, j: (0, 0))
    gate_cols = lambda b, i, j: (0, j)
    val_cols = lambda b, i, j: (0, nj + j)
    return pl.pallas_call(
        functools.partial(_ffn_kernel, tm=tm, n_tiles=n_tiles, nj=nj),
        grid=(bsz, n_tiles, nj),
        in_specs=[
            pl.BlockSpec((1, tm, d), lambda b, i, j: (b, i, 0)),
            pl.BlockSpec((1, SUBLANES_F32, d), lambda b, i, j: (b, jnp.maximum(i * hb - 1, 0), 0)),
            pl.BlockSpec((1, SUBLANES_F32, d),
                         lambda b, i, j: (b, jnp.minimum((i + 1) * hb, n_hblk - 1), 0)),
            pl.BlockSpec((1, SUBLANES_F32, d), lambda b, i, j: (b, N_META // SUBLANES_F32 - 1, 0)),
            pl.BlockSpec((d, tn), gate_cols), pl.BlockSpec((d, tn), val_cols),
            pl.BlockSpec((1, tn), gate_cols), pl.BlockSpec((1, tn), val_cols),
            pl.BlockSpec((FFN_CONV_WIDTH, tn), gate_cols), pl.BlockSpec((FFN_CONV_WIDTH, tn), val_cols),
            pl.BlockSpec((1, tn), gate_cols), pl.BlockSpec((1, tn), val_cols),
            pl.BlockSpec((tn, d), lambda b, i, j: (j, 0)),
            vec_d, vec_d, vec_d,
        ],
        out_specs=pl.BlockSpec((1, tm, d), lambda b, i, j: (b, i, 0)),
        out_shape=jax.ShapeDtypeStruct((bsz, t, d), F32),
        scratch_shapes=[pltpu.VMEM((tm + 2 * SUBLANES_BF16, d), BF16), pltpu.VMEM((tm, d), F32),
                        pltpu.VMEM((tm + 2 * SUBLANES_BF16, tn), F32),
                        pltpu.VMEM((tm + 2 * SUBLANES_BF16, tn), F32)],
        compiler_params=pltpu.CompilerParams(
            dimension_semantics=("parallel", "parallel", "arbitrary"),
            vmem_limit_bytes=VMEM_LIMIT),
        name="ffn",
    )(h1, h1, h1, h1_meta, w_up, w_up, b_up, b_up, cw, cw, cb, cb, w_down, b_down, g2, b2)


TILE_M = 512
TILE_Q = 1024
TILE_K = 1024
TILE_FF = 512


def _trunk(x, meta_frame, p):
    bsz, t, d = x.shape
    tm, tq, tk = min(t, TILE_M), min(t, TILE_Q), min(t, TILE_K)
    assert t % tm == 0 and t % tq == 0 and t % tk == 0, (t, tm, tq, tk)
    q_m, k_m, v_m, c_m = meta_frame
    ident = lambda b: b
    zero = lambda b: 0

    q, k, v, c = _inproj(x, N_META, p["ge"], p["be"], p["w_in"], p["b_in"], tm=tm)
    att = _attention(q, k, v, k_m, v_m, p["lam"], p["subln_g"], bsz=bsz, tq=tq, tk=tk)
    att_m = _attention(q_m, k, v, k_m, v_m, p["lam"], p["subln_g"],
                       bsz=bsz, tq=META_ROWS, tk=tk)
    mix_w = (p["ge"], p["be"], p["conv_w"], p["conv_b"], p["conv_g"], p["conv_beta"],
             p["w_out"], p["b_out"], p["g1"], p["b1"])
    h1 = _mix(x, att, c, c, c_m, *mix_w, bsz=bsz, t=t, tm=tm,
              xb=ident, ab=ident, cmb=ident, zero_right_at_end=True)
    h1_m = _mix(p["meta"], att_m, c_m, c, jnp.zeros((1, HALO, D_ATT), BF16), *mix_w,
                bsz=bsz, t=N_META, tm=N_META, xb=zero, ab=ident, cmb=zero,
                zero_right_at_end=False)
    return _ffn(h1, h1_m, p["w_up"], p["b_up"], p["ffn_cw"], p["ffn_cb"], p["w_down"],
                p["b_down"], p["g2"], p["b2"], tm=tm, tn=TILE_FF)


def kernel(x_prompt, x_sample, meta_tokens, ln_emb_g, ln_emb_b, w_in, b_in, lambda_q1, lambda_k1,
           lambda_q2, lambda_k2, subln_g, conv_w, conv_b, conv_ln_g, conv_ln_b, w_out, b_out,
           ln1_g, ln1_b, w_up, b_up, ffn_conv_w, ffn_conv_b, w_down, b_down, ln2_g, ln2_b):
    l = 0
    row = lambda v: v.reshape(1, -1).astype(F32)
    p = dict(
        meta=meta_tokens[None].astype(F32),
        ge=row(ln_emb_g), be=row(ln_emb_b),
        w_in=w_in[l].astype(BF16), b_in=row(b_in[l]),
        lam=(row(lambda_q1[l]), row(lambda_k1[l]), row(lambda_q2[l]), row(lambda_k2[l])),
        subln_g=subln_g[l].reshape(-1, 1).astype(F32),
        conv_w=conv_w[l].astype(F32), conv_b=row(conv_b[l]),
        conv_g=row(conv_ln_g[l]), conv_beta=row(conv_ln_b[l]),
        w_out=w_out[l].astype(BF16), b_out=row(b_out[l]), g1=row(ln1_g[l]), b1=row(ln1_b[l]),
        w_up=w_up[l].astype(BF16), b_up=row(b_up[l]),
        ffn_cw=ffn_conv_w[l].astype(F32), ffn_cb=row(ffn_conv_b[l]),
        w_down=w_down[l].astype(BF16), b_down=row(b_down[l]), g2=row(ln2_g[l]), b2=row(ln2_b[l]),
    )
    d = meta_tokens.shape[-1]
    meta_pad = jnp.zeros((1, META_ROWS, d), F32).at[0, :N_META].set(meta_tokens.astype(F32))
    meta_frame = _inproj(meta_pad, 0, p["ge"], p["be"], p["w_in"], p["b_in"], tm=META_ROWS)
    return (_trunk(x_prompt, meta_frame, p), _trunk(x_sample, meta_frame, p))
```
